```python
import math
import jax, jax.numpy as jnp
from jax import lax
import numpy as np

D_MODEL = 1024
BATCH = 2
SEQ = 8192
DEPTH = 1

CHUNK = 64
N_META = 16
QBLK = 128
SB_HEADS = 8
SB_HEAD_DIM = D_MODEL // SB_HEADS
DF_HEADS = 8
DF_HEAD_DIM = D_MODEL // (2 * DF_HEADS)
SB_W = SB_HEADS * SB_HEAD_DIM
DF_QK_W = DF_HEADS * 2 * DF_HEAD_DIM
DF_V_W = DF_HEADS * 2 * DF_HEAD_DIM
IN_W = 3 * SB_W + 2 * DF_QK_W + DF_V_W
D_FF = -(-(8 * D_MODEL) // (3 * 256)) * 256
EPS = 1e-6

kernel_name = 'hybrid_stickbreak_diffattn_block'


def rmsnorm(x, g):
    xf = x.astype(jnp.float32)
    y = xf * lax.rsqrt(jnp.mean(xf * xf, axis=-1, keepdims=True) + EPS)
    return (y * g.astype(jnp.float32)).astype(x.dtype)


def stick_breaking_block(q, k, v, qpos, kpos):
    z = jnp.einsum('bhqd,bhkd->bhqk', q, k).astype(jnp.float32) * (SB_HEAD_DIM ** -0.5)
    causal = kpos[None, :] < qpos[:, None]
    u = jnp.where(causal, jax.nn.log_sigmoid(-z), 0.0)
    rest = lax.cumsum(u, axis=3, reverse=True) - u
    log_a = jax.nn.log_sigmoid(z) + rest
    a = jnp.where(causal, jnp.exp(log_a), 0.0)
    return jnp.einsum('bhqk,bhkd->bhqd', a.astype(v.dtype), v)


def diff_attn_block(q, k, v, lam, slopes, qpos, kpos, qcid, kcid):
    s = jnp.einsum('bhcqd,bhckd->bhcqk', q, k).astype(jnp.float32) * (DF_HEAD_DIM ** -0.5)
    dist = jnp.abs(qpos[:, None] - kpos[None, :]).astype(jnp.float32)
    bias = -slopes[:, None, None] * dist
    mask = kcid[None, :] <= qcid[:, None]
    s = jnp.where(mask, s + bias[None, :, None], -jnp.inf)
    p = jax.nn.softmax(s, axis=-1)
    w = p[:, :, 0] - lam * p[:, :, 1]
    return jnp.einsum('bhqk,bhkd->bhqd', w.astype(v.dtype), v)


def setup_inputs(seed: int = 0) -> dict:
    key = jax.random.key(seed)
    ks = jax.random.split(key, 20)
    f32 = jnp.float32
    nrm = lambda k, shape, scale: (jax.random.normal(k, shape, f32) * scale)
    gain = lambda k, shape: 1.0 + 0.05 * jax.random.normal(k, shape, f32)
    return {
        'x': jax.random.normal(ks[0], (BATCH, SEQ, D_MODEL), f32),
        'meta': nrm(ks[1], (N_META, D_MODEL), 1.0),
        'norm_mix_g': gain(ks[2], (DEPTH, D_MODEL)),
        'w_in': nrm(ks[3], (DEPTH, D_MODEL, IN_W), D_MODEL ** -0.5),
        'w_gate': nrm(ks[4], (DEPTH, D_MODEL, 2 * D_MODEL), D_MODEL ** -0.5),
        'b_gate': nrm(ks[5], (DEPTH, 2 * D_MODEL), 0.1),
        'lam_q1': nrm(ks[6], (DEPTH, DF_HEAD_DIM), 0.1),
        'lam_k1': nrm(ks[7], (DEPTH, DF_HEAD_DIM), 0.1),
        'lam_q2': nrm(ks[8], (DEPTH, DF_HEAD_DIM), 0.1),
        'lam_k2': nrm(ks[9], (DEPTH, DF_HEAD_DIM), 0.1),
        'subln_g': gain(ks[10], (DEPTH, 2 * DF_HEAD_DIM)),
        'w_br_sb': nrm(ks[11], (DEPTH, SB_W, D_MODEL), SB_W ** -0.5),
        'w_br_df': nrm(ks[12], (DEPTH, DF_V_W, D_MODEL), DF_V_W ** -0.5),
        'w_out': nrm(ks[13], (DEPTH, D_MODEL, D_MODEL), D_MODEL ** -0.5),
        'norm_ffn_g': gain(ks[14], (DEPTH, D_MODEL)),
        'w_ffn_gate': nrm(ks[15], (DEPTH, D_MODEL, D_FF), D_MODEL ** -0.5),
        'w_ffn_up': nrm(ks[16], (DEPTH, D_MODEL, D_FF), D_MODEL ** -0.5),
        'w_ffn_down': nrm(ks[17], (DEPTH, D_FF, D_MODEL), D_FF ** -0.5),
        'norm_final_g': gain(ks[18], (D_MODEL,)),
    }


def reference(x, meta, norm_mix_g, w_in, w_gate, b_gate, lam_q1, lam_k1, lam_q2, lam_k2,
              subln_g, w_br_sb, w_br_df, w_out, norm_ffn_g, w_ffn_gate, w_ffn_up,
              w_ffn_down, norm_final_g):
    B, S, D = x.shape
    L = N_META + S
    L_pad = -(-L // QBLK) * QBLK
    h = jnp.concatenate([
        jnp.broadcast_to(meta[None].astype(x.dtype), (B, N_META, D)),
        x,
        jnp.zeros((B, L_pad - L, D), x.dtype)], axis=1)
    pos = jnp.arange(L_pad)
    cid = jnp.where(pos < N_META, 0, (pos - N_META) // CHUNK + 1)
    slopes = jnp.exp2(-8.0 * (jnp.arange(DF_HEADS) + 1) / DF_HEADS).astype(jnp.float32)
    splits = [SB_W, 2 * SB_W, 3 * SB_W, 3 * SB_W + DF_QK_W, 3 * SB_W + 2 * DF_QK_W]

    for l in range(DEPTH):
        lam_init = 0.8 - 0.6 * math.exp(-0.3 * l)
        xn = rmsnorm(h, norm_mix_g[l])
        proj = xn @ w_in[l]
        sb_q, sb_k, sb_v, df_q, df_k, df_v = jnp.split(proj, splits, axis=-1)
        to_heads = lambda t, nh, dh: t.reshape(B, L_pad, nh, dh).transpose(0, 2, 1, 3)
        sq = to_heads(sb_q, SB_HEADS, SB_HEAD_DIM)
        sk = to_heads(sb_k, SB_HEADS, SB_HEAD_DIM)
        sv = to_heads(sb_v, SB_HEADS, SB_HEAD_DIM)
        dq = df_q.reshape(B, L_pad, DF_HEADS, 2, DF_HEAD_DIM).transpose(0, 2, 3, 1, 4)
        dk = df_k.reshape(B, L_pad, DF_HEADS, 2, DF_HEAD_DIM).transpose(0, 2, 3, 1, 4)
        dv = to_heads(df_v, DF_HEADS, 2 * DF_HEAD_DIM)
        lam = (jnp.exp(jnp.sum(lam_q1[l].astype(jnp.float32) * lam_k1[l].astype(jnp.float32)))
               - jnp.exp(jnp.sum(lam_q2[l].astype(jnp.float32) * lam_k2[l].astype(jnp.float32)))
               + lam_init)

        sb_blocks = []
        df_blocks = []
        for i in range(L_pad // QBLK):
            q0 = i * QBLK
            qe = q0 + QBLK
            lk = min(qe + CHUNK, L_pad)
            sb_blocks.append(stick_breaking_block(
                sq[:, :, q0:qe], sk[:, :, :qe], sv[:, :, :qe], pos[q0:qe], pos[:qe]))
            df_blocks.append(diff_attn_block(
                dq[:, :, :, q0:qe], dk[:, :, :, :lk], dv[:, :, :lk], lam, slopes,
                pos[q0:qe], pos[:lk], cid[q0:qe], cid[:lk]))
        o_sb = jnp.concatenate(sb_blocks, axis=2).transpose(0, 2, 1, 3).reshape(B, L_pad, SB_W)
        o_df = jnp.concatenate(df_blocks, axis=2)
        o_df = (rmsnorm(o_df, subln_g[l]) * (1.0 - lam_init)).transpose(0, 2, 1, 3).reshape(B, L_pad, DF_V_W)

        y_sb = o_sb @ w_br_sb[l]
        y_df = o_df @ w_br_df[l]
        gates = jax.nn.sigmoid(xn @ w_gate[l] + b_gate[l]).reshape(B, L_pad, 2, D)
        merged = gates[:, :, 0] * y_sb + gates[:, :, 1] * y_df
        h = h + merged @ w_out[l]

        hn = rmsnorm(h, norm_ffn_g[l])
        h = h + (jax.nn.silu(hn @ w_ffn_gate[l]) * (hn @ w_ffn_up[l])) @ w_ffn_down[l]

    y = rmsnorm(h, norm_final_g)
    return y[:, N_META:N_META + S]
```

```python
import functools
import math

import jax
import jax.numpy as jnp
from jax import lax
from jax.experimental import pallas as pl
from jax.experimental.pallas import tpu as pltpu

D_MODEL = 1024
N_META = 16
CHUNK = 64
HEADS = 8
HEAD_W = 128
DF_HALF = 64
IN_W = 6 * D_MODEL
D_FF = 2816
EPS = 1e-6
LAM_INIT = 0.8 - 0.6 * math.exp(-0.3 * 0)
SB_SCALE = HEAD_W ** -0.5
DF_SCALE = DF_HALF ** -0.5

SBQ, SBK, SBV, DFQ, DFK, DFV = 0, 8, 16, 24, 32, 40

META_PAD = 128
TQ = 256
TK = 256
TM = 512
FF_CHUNK = 256
VMEM_LIMIT = 48 * 1024 * 1024

_BF16 = jnp.bfloat16
_F32 = jnp.float32
_NT = (((1,), (1,)), ((), ()))


def _rms(xf, g):
    ms = jnp.mean(xf * xf, axis=-1, keepdims=True)
    return xf * lax.rsqrt(ms + EPS) * g


def _inproj_kernel(x_ref, g_ref, w_ref, o_ref, *, n_chunk):
    xn = _rms(x_ref[...], g_ref[...]).astype(_BF16)
    for n in range(IN_W // n_chunk):
        sl = slice(n * n_chunk, (n + 1) * n_chunk)
        o_ref[:, sl] = jnp.dot(xn, w_ref[:, sl], preferred_element_type=_F32).astype(_BF16)


def _in_proj(x2, g, w, tm):
    m = x2.shape[0]
    return pl.pallas_call(
        functools.partial(_inproj_kernel, n_chunk=512),
        out_shape=jax.ShapeDtypeStruct((m, IN_W), _BF16),
        grid=(m // tm,),
        in_specs=[
            pl.BlockSpec((tm, D_MODEL), lambda i: (i, 0)),
            pl.BlockSpec((1, D_MODEL), lambda i: (0, 0)),
            pl.BlockSpec((D_MODEL, IN_W), lambda i: (0, 0), pipeline_mode=pl.Buffered(1)),
        ],
        out_specs=pl.BlockSpec((tm, IN_W), lambda i: (i, 0)),
        compiler_params=pltpu.CompilerParams(
            dimension_semantics=("arbitrary",), vmem_limit_bytes=VMEM_LIMIT),
        name="in_proj",
    )(x2, g, w)


def _sb_kernel(q_ref, k_ref, v_ref, km_ref, vm_ref, o_ref, acc_ref, c_ref):
    i = pl.program_id(2)
    q = q_ref[...]

    row = lax.broadcasted_iota(jnp.int32, (TQ, TK), 0)
    col = lax.broadcasted_iota(jnp.int32, (TQ, TK), 1)
    tri = (row > col).astype(_BF16)

    def block(kt, vt, mask, tri_m, first):
        z = lax.dot_general(q, kt, _NT, preferred_element_type=_F32) * SB_SCALE
        t = jnp.log(1.0 + jnp.exp(-jnp.abs(z)))
        sp = jnp.maximum(z, 0.0) + t
        lsig = jnp.minimum(z, 0.0) - t
        if mask is not None:
            sp = jnp.where(mask, sp, 0.0)
        suffix = jnp.dot(sp.astype(_BF16), tri_m, preferred_element_type=_F32)
        la = lsig - suffix
        if not first:
            la = la - c_ref[...]
        a = jnp.exp(la)
        if mask is not None:
            a = jnp.where(mask, a, 0.0)
        pv = jnp.dot(a.astype(_BF16), vt, preferred_element_type=_F32)
        csum = jnp.sum(sp, axis=1, keepdims=True)
        if first:
            acc_ref[...] = pv
            c_ref[...] = csum
        else:
            acc_ref[...] += pv
            c_ref[...] += csum

    d0 = pl.multiple_of(i * TK, TK)
    block(k_ref[pl.ds(d0, TK), :], v_ref[pl.ds(d0, TK), :], col < row, tri, True)

    def body(jj, carry):
        k0 = pl.multiple_of((i - 1 - jj) * TK, TK)
        block(k_ref[pl.ds(k0, TK), :], v_ref[pl.ds(k0, TK), :], None, tri, False)
        return carry

    lax.fori_loop(0, i, body, 0)

    mrow = lax.broadcasted_iota(jnp.int32, (META_PAD, META_PAD), 0)
    mcol = lax.broadcasted_iota(jnp.int32, (META_PAD, META_PAD), 1)
    valid = lax.broadcasted_iota(jnp.int32, (TQ, META_PAD), 1) < N_META
    block(km_ref[...], vm_ref[...], valid, (mrow > mcol).astype(_BF16), False)

    o_ref[...] = acc_ref[...].astype(o_ref.dtype)


def _sb_attention(proj, meta_proj, b, s):
    return pl.pallas_call(
        _sb_kernel,
        out_shape=jax.ShapeDtypeStruct((b, s, D_MODEL), _BF16),
        grid=(b, HEADS, s // TQ),
        in_specs=[
            pl.BlockSpec((None, TQ, HEAD_W), lambda bi, h, i: (bi, i, SBQ + h)),
            pl.BlockSpec((None, s, HEAD_W), lambda bi, h, i: (bi, 0, SBK + h)),
            pl.BlockSpec((None, s, HEAD_W), lambda bi, h, i: (bi, 0, SBV + h)),
            pl.BlockSpec((META_PAD, HEAD_W), lambda bi, h, i: (0, SBK + h)),
            pl.BlockSpec((META_PAD, HEAD_W), lambda bi, h, i: (0, SBV + h)),
        ],
        out_specs=pl.BlockSpec((None, TQ, HEAD_W), lambda bi, h, i: (bi, i, h)),
        scratch_shapes=[
            pltpu.VMEM((TQ, HEAD_W), _F32),
            pltpu.VMEM((TQ, 1), _F32),
        ],
        compiler_params=pltpu.CompilerParams(
            dimension_semantics=("arbitrary", "arbitrary", "arbitrary"),
            vmem_limit_bytes=VMEM_LIMIT),
        name="sb_attn",
    )(proj, proj, proj, meta_proj, meta_proj)


def _df_kernel(slopes_ref, lam_ref, g_ref, q_ref, k_ref, v_ref, km_ref, vm_ref, o_ref,
               acc1, acc2, m1, m2, l1, l2, bias_ref):
    h = pl.program_id(1)
    i = pl.program_id(2)
    slope = slopes_ref[h]

    q = q_ref[...]
    lane = lax.broadcasted_iota(jnp.int32, (TQ, HEAD_W), 1)
    zero = jnp.zeros_like(q)
    q_lo = jnp.where(lane < DF_HALF, q, zero)
    q_hi = jnp.where(lane >= DF_HALF, q, zero)

    row = lax.broadcasted_iota(jnp.int32, (TQ, TK), 0)
    col = lax.broadcasted_iota(jnp.int32, (TQ, TK), 1)
    dt = (row - col).astype(_F32)
    bias_ref[...] = -slope * dt

    def update(x, vt, cb, acc, m, l, first):
        mx = jnp.max(x, axis=1, keepdims=True) + cb
        if first:
            m_new = mx
        else:
            m_old = m[...]
            m_new = jnp.maximum(m_old, mx)
        p = jnp.exp(x - (m_new - cb))
        ps = jnp.sum(p, axis=1, keepdims=True)
        pv = jnp.dot(p.astype(_BF16), vt, preferred_element_type=_F32)
        if first:
            l[...] = ps
            acc[...] = pv
        else:
            alpha = jnp.exp(m_old - m_new)
            l[...] = alpha * l[...] + ps
            acc[...] = alpha * acc[...] + pv
        m[...] = m_new

    def block(kt, vt, bias, mask, cb, first):
        s1 = lax.dot_general(q_lo, kt, _NT, preferred_element_type=_F32)
        s2 = lax.dot_general(q_hi, kt, _NT, preferred_element_type=_F32)
        x1 = s1 * DF_SCALE + bias
        x2 = s2 * DF_SCALE + bias
        if mask is not None:
            x1 = jnp.where(mask, x1, -jnp.inf)
            x2 = jnp.where(mask, x2, -jnp.inf)
        update(x1, vt, cb, acc1, m1, l1, first)
        update(x2, vt, cb, acc2, m2, l2, first)

    d0 = pl.multiple_of(i * TK, TK)
    chunk_end = (row // CHUNK + 1) * CHUNK
    block(k_ref[pl.ds(d0, TK), :], v_ref[pl.ds(d0, TK), :],
          -slope * jnp.abs(dt), col < chunk_end, 0.0, True)

    def body(jj, carry):
        k0 = pl.multiple_of((i - 1 - jj) * TK, TK)
        cb = -slope * ((jj + 1) * TK).astype(_F32)
        block(k_ref[pl.ds(k0, TK), :], v_ref[pl.ds(k0, TK), :], bias_ref[...], None, cb, False)
        return carry

    lax.fori_loop(0, i, body, 0)

    cb_meta = -slope * (i * TQ + N_META).astype(_F32)
    valid = lax.broadcasted_iota(jnp.int32, (TQ, META_PAD), 1) < N_META
    block(km_ref[...], vm_ref[...], bias_ref[:, :META_PAD], valid, cb_meta, False)

    lv = lam_ref[...]
    d1 = jnp.sum(lv[0:1, :] * lv[1:2, :], axis=1, keepdims=True)
    d2 = jnp.sum(lv[2:3, :] * lv[3:4, :], axis=1, keepdims=True)
    lam = jnp.exp(d1) - jnp.exp(d2) + LAM_INIT
    o = acc1[...] * (1.0 / l1[...]) - lam * (acc2[...] * (1.0 / l2[...]))
    o_ref[...] = (_rms(o, g_ref[...]) * (1.0 - LAM_INIT)).astype(o_ref.dtype)


def _df_attention(proj, meta_proj, slopes, lamvec, subln_g, b, s):
    return pl.pallas_call(
        _df_kernel,
        out_shape=jax.ShapeDtypeStruct((b, s, D_MODEL), _BF16),
        grid=(b, HEADS, s // TQ),
        in_specs=[
            pl.BlockSpec(memory_space=pltpu.SMEM),
            pl.BlockSpec((4, DF_HALF), lambda bi, h, i: (0, 0)),
            pl.BlockSpec((1, HEAD_W), lambda bi, h, i: (0, 0)),
            pl.BlockSpec((None, TQ, HEAD_W), lambda bi, h, i: (bi, i, DFQ + h)),
            pl.BlockSpec((None, s, HEAD_W), lambda bi, h, i: (bi, 0, DFK + h)),
            pl.BlockSpec((None, s, HEAD_W), lambda bi, h, i: (bi, 0, DFV + h)),
            pl.BlockSpec((META_PAD, HEAD_W), lambda bi, h, i: (0, DFK + h)),
            pl.BlockSpec((META_PAD, HEAD_W), lambda bi, h, i: (0, DFV + h)),
        ],
        out_specs=pl.BlockSpec((None, TQ, HEAD_W), lambda bi, h, i: (bi, i, h)),
        scratch_shapes=[
            pltpu.VMEM((TQ, HEAD_W), _F32),
            pltpu.VMEM((TQ, HEAD_W), _F32),
            pltpu.VMEM((TQ, 1), _F32),
            pltpu.VMEM((TQ, 1), _F32),
            pltpu.VMEM((TQ, 1), _F32),
            pltpu.VMEM((TQ, 1), _F32),
            pltpu.VMEM((TQ, TK), _F32),
        ],
        compiler_params=pltpu.CompilerParams(
            dimension_semantics=("arbitrary", "arbitrary", "arbitrary"),
            vmem_limit_bytes=VMEM_LIMIT),
        name="df_attn",
    )(slopes, lamvec, subln_g, proj, proj, proj, meta_proj, meta_proj)


def _merge_kernel(x_ref, g_ref, osb_ref, odf_ref, wg_ref, bg_ref, wsb_ref, wdf_ref, wo_ref,
                  h_ref, merged_ref, *, n_chunk):
    x = x_ref[...]
    xn = _rms(x, g_ref[...]).astype(_BF16)
    osb = osb_ref[...]
    odf = odf_ref[...]
    for n in range(D_MODEL // n_chunk):
        sa = slice(n * n_chunk, (n + 1) * n_chunk)
        sb = slice(D_MODEL + n * n_chunk, D_MODEL + (n + 1) * n_chunk)
        ga = jax.nn.sigmoid(jnp.dot(xn, wg_ref[:, sa], preferred_element_type=_F32) + bg_ref[:, sa])
        gb = jax.nn.sigmoid(jnp.dot(xn, wg_ref[:, sb], preferred_element_type=_F32) + bg_ref[:, sb])
        ysb = jnp.dot(osb, wsb_ref[:, sa], preferred_element_type=_F32)
        ydf = jnp.dot(odf, wdf_ref[:, sa], preferred_element_type=_F32)
        merged_ref[:, sa] = (ga * ysb + gb * ydf).astype(_BF16)
    h_ref[...] = x + jnp.dot(merged_ref[...], wo_ref[...], preferred_element_type=_F32)


def _merge(x2, g, osb, odf, wg, bg, wsb, wdf, wo):
    m = x2.shape[0]
    const = lambda shape: pl.BlockSpec(shape, lambda i: (0, 0), pipeline_mode=pl.Buffered(1))
    rows = lambda: pl.BlockSpec((TM, D_MODEL), lambda i: (i, 0))
    return pl.pallas_call(
        functools.partial(_merge_kernel, n_chunk=512),
        out_shape=jax.ShapeDtypeStruct((m, D_MODEL), _F32),
        grid=(m // TM,),
        in_specs=[
            rows(), const((1, D_MODEL)), rows(), rows(),
            const((D_MODEL, 2 * D_MODEL)), const((1, 2 * D_MODEL)),
            const((D_MODEL, D_MODEL)), const((D_MODEL, D_MODEL)), const((D_MODEL, D_MODEL)),
        ],
        out_specs=rows(),
        scratch_shapes=[pltpu.VMEM((TM, D_MODEL), _BF16)],
        compiler_params=pltpu.CompilerParams(
            dimension_semantics=("arbitrary",), vmem_limit_bytes=VMEM_LIMIT),
        name="merge",
    )(x2, g, osb, odf, wg, bg, wsb, wdf, wo)


def _ffn_kernel(h_ref, gf_ref, wg_ref, wu_ref, wd_ref, gl_ref, y_ref, acc_ref):
    h = h_ref[...]
    hn = _rms(h, gf_ref[...]).astype(_BF16)
    for n in range(D_FF // FF_CHUNK):
        sl = slice(n * FF_CHUNK, (n + 1) * FF_CHUNK)
        gate = jnp.dot(hn, wg_ref[:, sl], preferred_element_type=_F32)
        up = jnp.dot(hn, wu_ref[:, sl], preferred_element_type=_F32)
        act = (jax.nn.silu(gate) * up).astype(_BF16)
        down = jnp.dot(act, wd_ref[sl, :], preferred_element_type=_F32)
        if n == 0:
            acc_ref[...] = down
        else:
            acc_ref[...] += down
    y_ref[...] = _rms(h + acc_ref[...], gl_ref[...])


def _ffn(h1, gf, wg, wu, wd, gl):
    m = h1.shape[0]
    const = lambda shape: pl.BlockSpec(shape, lambda i: (0, 0), pipeline_mode=pl.Buffered(1))
    rows = lambda: pl.BlockSpec((TM, D_MODEL), lambda i: (i, 0))
    return pl.pallas_call(
        _ffn_kernel,
        out_shape=jax.ShapeDtypeStruct((m, D_MODEL), _F32),
        grid=(m // TM,),
        in_specs=[
            rows(), const((1, D_MODEL)),
            const((D_MODEL, D_FF)), const((D_MODEL, D_FF)), const((D_FF, D_MODEL)),
            const((1, D_MODEL)),
        ],
        out_specs=rows(),
        scratch_shapes=[pltpu.VMEM((TM, D_MODEL), _F32)],
        compiler_params=pltpu.CompilerParams(
            dimension_semantics=("arbitrary",), vmem_limit_bytes=VMEM_LIMIT),
        name="ffn",
    )(h1, gf, wg, wu, wd, gl)


def kernel(x, meta, norm_mix_g, w_in, w_gate, b_gate, lam_q1, lam_k1, lam_q2, lam_k2, subln_g,
           w_br_sb, w_br_df, w_out, norm_ffn_g, w_ffn_gate, w_ffn_up, w_ffn_down, norm_final_g):
    b, s, d = x.shape
    assert d == D_MODEL and s % TQ == 0 and (b * s) % TM == 0 and TQ == TK
    assert meta.shape == (N_META, D_MODEL) and w_in.shape[0] == 1

    bf = lambda w: w.astype(_BF16)
    x2 = x.reshape(b * s, d)
    g_mix = norm_mix_g[0][None, :]
    w_in_b = bf(w_in[0])

    proj = _in_proj(x2, g_mix, w_in_b, TM).reshape(b, s, IN_W)
    meta_proj = _in_proj(meta.astype(x.dtype), g_mix, w_in_b, N_META)
    meta_proj = jnp.pad(meta_proj, ((0, META_PAD - N_META), (0, 0)))

    slopes = jnp.exp2(-8.0 * (jnp.arange(HEADS) + 1) / HEADS).astype(_F32)
    lamvec = jnp.stack([lam_q1[0], lam_k1[0], lam_q2[0], lam_k2[0]]).astype(_F32)

    o_sb = _sb_attention(proj, meta_proj, b, s).reshape(b * s, d)
    o_df = _df_attention(proj, meta_proj, slopes, lamvec, subln_g[0][None, :], b, s).reshape(b * s, d)

    h1 = _merge(x2, g_mix, o_sb, o_df, bf(w_gate[0]), b_gate[0][None, :],
                bf(w_br_sb[0]), bf(w_br_df[0]), bf(w_out[0]))
    y = _ffn(h1, norm_ffn_g[0][None, :], bf(w_ffn_gate[0]), bf(w_ffn_up[0]), bf(w_ffn_down[0]),
             norm_final_g[None, :])
    return y.reshape(b, s, d)
```

```python
import functools
import math

import jax
import jax.numpy as jnp
from jax import lax
from jax.experimental import pallas as pl
from jax.experimental.pallas import tpu as pltpu

D_MODEL = 1024
N_META = 16
CHUNK = 64
HEADS = 8
HEAD_W = 128
DF_HALF = 64
IN_W = 6 * D_MODEL
D_FF = 2816
EPS = 1e-6
LAM_INIT = 0.8 - 0.6 * math.exp(-0.3 * 0)
LOG2E = 1.4426950408889634
SB_SCALE = HEAD_W ** -0.5
DF_SCALE2 = DF_HALF ** -0.5 * LOG2E

SBQ, SBK, SBV, DFQ, DFK, DFV = 0, 8, 16, 24, 32, 40

LANES = 128
META_PAD = LANES
SB_T = 256
DF_T = 512
TM = 512
FF_CHUNK = 256
VMEM_LIMIT = 48 * 1024 * 1024

SB_DEAD = 110.0

_BF16 = jnp.bfloat16
_F32 = jnp.float32
_NT = (((1,), (1,)), ((), ()))


def _rms(xf, g):
    ms = jnp.mean(xf * xf, axis=-1, keepdims=True)
    return xf * lax.rsqrt(ms + EPS) * g


def _inproj_kernel(x_ref, g_ref, w_ref, o_ref, *, n_chunk):
    xn = _rms(x_ref[...], g_ref[...]).astype(_BF16)
    for n in range(IN_W // n_chunk):
        sl = slice(n * n_chunk, (n + 1) * n_chunk)
        o_ref[:, sl] = jnp.dot(xn, w_ref[:, sl], preferred_element_type=_F32).astype(_BF16)


def _in_proj(x2, g, w, tm):
    m = x2.shape[0]
    return pl.pallas_call(
        functools.partial(_inproj_kernel, n_chunk=512),
        out_shape=jax.ShapeDtypeStruct((m, IN_W), _BF16),
        grid=(m // tm,),
        in_specs=[
            pl.BlockSpec((tm, D_MODEL), lambda i: (i, 0)),
            pl.BlockSpec((1, D_MODEL), lambda i: (0, 0)),
            pl.BlockSpec((D_MODEL, IN_W), lambda i: (0, 0), pipeline_mode=pl.Buffered(1)),
        ],
        out_specs=pl.BlockSpec((tm, IN_W), lambda i: (i, 0)),
        compiler_params=pltpu.CompilerParams(
            dimension_semantics=("arbitrary",), vmem_limit_bytes=VMEM_LIMIT),
        name="in_proj",
    )(x2, g, w)


def _sb_kernel(q_ref, k_ref, v_ref, km_ref, vm_ref, o_ref, acc_ref, c_ref):
    t_blk = SB_T
    i = pl.program_id(2)
    q = q_ref[...]

    row = lax.broadcasted_iota(jnp.int32, (t_blk, t_blk), 0)
    col = lax.broadcasted_iota(jnp.int32, (t_blk, t_blk), 1)
    tri = (row > col).astype(_BF16)

    def block(kt, vt, mask, tri_m, first):
        z = lax.dot_general(q, kt, _NT, preferred_element_type=_F32) * SB_SCALE
        t = jnp.log(1.0 + jnp.exp2(jnp.abs(z) * (-LOG2E)))
        sp = jnp.maximum(z, 0.0) + t
        lsig = jnp.minimum(z, 0.0) - t
        if mask is not None:
            sp = jnp.where(mask, sp, 0.0)
        suffix = jnp.dot(sp.astype(_BF16), tri_m, preferred_element_type=_F32)
        a = jnp.exp(lsig - suffix)
        if mask is not None:
            a = jnp.where(mask, a, 0.0)
        pv = jnp.dot(a.astype(_BF16), vt, preferred_element_type=_F32)
        csum = jnp.broadcast_to(jnp.sum(sp, axis=1, keepdims=True), (t_blk, LANES))
        if first:
            acc_ref[...] = pv
            c_new = csum
        else:
            c_old = c_ref[...]
            acc_ref[...] += jnp.exp(-c_old) * pv
            c_new = c_old + csum
        c_ref[...] = c_new
        return jnp.min(c_new)

    d0 = pl.multiple_of(i * t_blk, t_blk)
    cmin0 = block(k_ref[pl.ds(d0, t_blk), :], v_ref[pl.ds(d0, t_blk), :], col < row, tri, True)

    def cond(carry):
        jj, cmin = carry
        return jnp.logical_and(jj < i, cmin < SB_DEAD)

    def body(carry):
        jj, _ = carry
        k0 = pl.multiple_of((i - 1 - jj) * t_blk, t_blk)
        cmin = block(k_ref[pl.ds(k0, t_blk), :], v_ref[pl.ds(k0, t_blk), :], None, tri, False)
        return jj + 1, cmin

    _, cmin = lax.while_loop(cond, body, (jnp.int32(0), cmin0))

    @pl.when(cmin < SB_DEAD)
    def _():
        mrow = lax.broadcasted_iota(jnp.int32, (META_PAD, META_PAD), 0)
        mcol = lax.broadcasted_iota(jnp.int32, (META_PAD, META_PAD), 1)
        valid = lax.broadcasted_iota(jnp.int32, (t_blk, META_PAD), 1) < N_META
        block(km_ref[...], vm_ref[...], valid, (mrow > mcol).astype(_BF16), False)

    o_ref[...] = acc_ref[...].astype(o_ref.dtype)


def _sb_attention(proj, meta_proj, b, s):
    t_blk = SB_T
    return pl.pallas_call(
        _sb_kernel,
        out_shape=jax.ShapeDtypeStruct((b, s, D_MODEL), _BF16),
        grid=(b, HEADS, s // t_blk),
        in_specs=[
            pl.BlockSpec((None, t_blk, HEAD_W), lambda bi, h, i: (bi, i, SBQ + h)),
            pl.BlockSpec((None, s, HEAD_W), lambda bi, h, i: (bi, 0, SBK + h)),
            pl.BlockSpec((None, s, HEAD_W), lambda bi, h, i: (bi, 0, SBV + h)),
            pl.BlockSpec((META_PAD, HEAD_W), lambda bi, h, i: (0, SBK + h)),
            pl.BlockSpec((META_PAD, HEAD_W), lambda bi, h, i: (0, SBV + h)),
        ],
        out_specs=pl.BlockSpec((None, t_blk, HEAD_W), lambda bi, h, i: (bi, i, h)),
        scratch_shapes=[
            pltpu.VMEM((t_blk, HEAD_W), _F32),
            pltpu.VMEM((t_blk, LANES), _F32),
        ],
        compiler_params=pltpu.CompilerParams(
            dimension_semantics=("arbitrary", "arbitrary", "arbitrary"),
            vmem_limit_bytes=VMEM_LIMIT),
        name="sb_attn",
    )(proj, proj, proj, meta_proj, meta_proj)


def _df_kernel(slopes_ref, lam_ref, g_ref, q_ref, k_ref, v_ref, km_ref, vm_ref, o_ref,
               acc1, acc2, m1, m2, bias_ref, vext_ref, vmext_ref):
    t_blk = DF_T
    h = pl.program_id(1)
    i = pl.program_id(2)
    slope2 = slopes_ref[h] * LOG2E

    @pl.when(i == 0)
    def _():
        vext_ref[:, :HEAD_W] = v_ref[...]
        vext_ref[:, HEAD_W:] = jnp.ones((vext_ref.shape[0], LANES), _BF16)
        vmext_ref[:, :HEAD_W] = vm_ref[...]
        vmext_ref[:, HEAD_W:] = jnp.ones((META_PAD, LANES), _BF16)
        row = lax.broadcasted_iota(jnp.int32, (t_blk, t_blk), 0)
        col = lax.broadcasted_iota(jnp.int32, (t_blk, t_blk), 1)
        bias_ref[...] = -slope2 * (row - col).astype(_F32)

    q = q_ref[...]
    lane = lax.broadcasted_iota(jnp.int32, (t_blk, HEAD_W), 1)
    zero = jnp.zeros_like(q)
    q_lo = jnp.where(lane < DF_HALF, q, zero)
    q_hi = jnp.where(lane >= DF_HALF, q, zero)

    def update(y, vt, cb, acc, m, first):
        reps = y.shape[1] // LANES
        mx = jnp.broadcast_to(jnp.max(y, axis=1, keepdims=True), (t_blk, LANES)) + cb
        if first:
            m_new = mx
        else:
            m_old = m[...]
            m_new = jnp.maximum(m_old, mx)
        shift = m_new - cb
        p = jnp.exp2(y - jnp.concatenate([shift] * reps, axis=1))
        pv = jnp.dot(p.astype(_BF16), vt, preferred_element_type=_F32)
        if first:
            acc[...] = pv
        else:
            alpha = jnp.exp2(m_old - m_new)
            acc[...] = jnp.concatenate([alpha, alpha], axis=1) * acc[...] + pv
        m[...] = m_new

    def block(kt, vt, bias, mask, cb, first):
        y1 = lax.dot_general(q_lo, kt, _NT, preferred_element_type=_F32) * DF_SCALE2 + bias
        y2 = lax.dot_general(q_hi, kt, _NT, preferred_element_type=_F32) * DF_SCALE2 + bias
        if mask is not None:
            y1 = jnp.where(mask, y1, -jnp.inf)
            y2 = jnp.where(mask, y2, -jnp.inf)
        update(y1, vt, cb, acc1, m1, first)
        update(y2, vt, cb, acc2, m2, first)

    d0 = pl.multiple_of(i * t_blk, t_blk)
    row = lax.broadcasted_iota(jnp.int32, (t_blk, t_blk), 0)
    col = lax.broadcasted_iota(jnp.int32, (t_blk, t_blk), 1)
    chunk_end = (row // CHUNK + 1) * CHUNK
    block(k_ref[pl.ds(d0, t_blk), :], vext_ref[pl.ds(d0, t_blk), :],
          -jnp.abs(bias_ref[...]), col < chunk_end, 0.0, True)

    def body(jj, carry):
        k0 = pl.multiple_of((i - 1 - jj) * t_blk, t_blk)
        cb = -slope2 * ((jj + 1) * t_blk).astype(_F32)
        block(k_ref[pl.ds(k0, t_blk), :], vext_ref[pl.ds(k0, t_blk), :], bias_ref[...], None, cb, False)
        return carry

    lax.fori_loop(0, i, body, 0)

    cb_meta = -slope2 * (i * t_blk + N_META).astype(_F32)
    valid = lax.broadcasted_iota(jnp.int32, (t_blk, META_PAD), 1) < N_META
    block(km_ref[...], vmext_ref[...], bias_ref[:, :META_PAD], valid, cb_meta, False)

    lv = lam_ref[...]
    d1 = jnp.sum(lv[0:1, :] * lv[1:2, :], axis=1, keepdims=True)
    d2 = jnp.sum(lv[2:3, :] * lv[3:4, :], axis=1, keepdims=True)
    lam = jnp.exp(d1) - jnp.exp(d2) + LAM_INIT
    a1 = acc1[...]
    a2 = acc2[...]
    o = a1[:, :HEAD_W] * (1.0 / a1[:, HEAD_W:]) - lam * (a2[:, :HEAD_W] * (1.0 / a2[:, HEAD_W:]))
    o_ref[...] = (_rms(o, g_ref[...]) * (1.0 - LAM_INIT)).astype(o_ref.dtype)


def _df_attention(proj, meta_proj, slopes, lamvec, subln_g, b, s):
    t_blk = DF_T
    return pl.pallas_call(
        _df_kernel,
        out_shape=jax.ShapeDtypeStruct((b, s, D_MODEL), _BF16),
        grid=(b, HEADS, s // t_blk),
        in_specs=[
            pl.BlockSpec(memory_space=pltpu.SMEM),
            pl.BlockSpec((4, DF_HALF), lambda bi, h, i: (0, 0)),
            pl.BlockSpec((1, HEAD_W), lambda bi, h, i: (0, 0)),
            pl.BlockSpec((None, t_blk, HEAD_W), lambda bi, h, i: (bi, i, DFQ + h)),
            pl.BlockSpec((None, s, HEAD_W), lambda bi, h, i: (bi, 0, DFK + h)),
            pl.BlockSpec((None, s, HEAD_W), lambda bi, h, i: (bi, 0, DFV + h)),
            pl.BlockSpec((META_PAD, HEAD_W), lambda bi, h, i: (0, DFK + h)),
            pl.BlockSpec((META_PAD, HEAD_W), lambda bi, h, i: (0, DFV + h)),
        ],
        out_specs=pl.BlockSpec((None, t_blk, HEAD_W), lambda bi, h, i: (bi, i, h)),
        scratch_shapes=[
            pltpu.VMEM((t_blk, HEAD_W + LANES), _F32),
            pltpu.VMEM((t_blk, HEAD_W + LANES), _F32),
            pltpu.VMEM((t_blk, LANES), _F32),
            pltpu.VMEM((t_blk, LANES), _F32),
            pltpu.VMEM((t_blk, t_blk), _F32),
            pltpu.VMEM((s, HEAD_W + LANES), _BF16),
            pltpu.VMEM((META_PAD, HEAD_W + LANES), _BF16),
        ],
        compiler_params=pltpu.CompilerParams(
            dimension_semantics=("arbitrary", "arbitrary", "arbitrary"),
            vmem_limit_bytes=VMEM_LIMIT),
        name="df_attn",
    )(slopes, lamvec, subln_g, proj, proj, proj, meta_proj, meta_proj)


def _merge_kernel(x_ref, g_ref, osb_ref, odf_ref, wg_ref, bg_ref, wsb_ref, wdf_ref, wo_ref,
                  h_ref, merged_ref, *, n_chunk):
    x = x_ref[...]
    xn = _rms(x, g_ref[...]).astype(_BF16)
    osb = osb_ref[...]
    odf = odf_ref[...]
    for n in range(D_MODEL // n_chunk):
        sa = slice(n * n_chunk, (n + 1) * n_chunk)
        sb = slice(D_MODEL + n * n_chunk, D_MODEL + (n + 1) * n_chunk)
        ga = jax.nn.sigmoid(jnp.dot(xn, wg_ref[:, sa], preferred_element_type=_F32) + bg_ref[:, sa])
        gb = jax.nn.sigmoid(jnp.dot(xn, wg_ref[:, sb], preferred_element_type=_F32) + bg_ref[:, sb])
        ysb = jnp.dot(osb, wsb_ref[:, sa], preferred_element_type=_F32)
        ydf = jnp.dot(odf, wdf_ref[:, sa], preferred_element_type=_F32)
        merged_ref[:, sa] = (ga * ysb + gb * ydf).astype(_BF16)
    h_ref[...] = x + jnp.dot(merged_ref[...], wo_ref[...], preferred_element_type=_F32)


def _merge(x2, g, osb, odf, wg, bg, wsb, wdf, wo):
    m = x2.shape[0]
    const = lambda shape: pl.BlockSpec(shape, lambda i: (0, 0), pipeline_mode=pl.Buffered(1))
    rows = lambda: pl.BlockSpec((TM, D_MODEL), lambda i: (i, 0))
    return pl.pallas_call(
        functools.partial(_merge_kernel, n_chunk=512),
        out_shape=jax.ShapeDtypeStruct((m, D_MODEL), _F32),
        grid=(m // TM,),
        in_specs=[
            rows(), const((1, D_MODEL)), rows(), rows(),
            const((D_MODEL, 2 * D_MODEL)), const((1, 2 * D_MODEL)),
            const((D_MODEL, D_MODEL)), const((D_MODEL, D_MODEL)), const((D_MODEL, D_MODEL)),
        ],
        out_specs=rows(),
        scratch_shapes=[pltpu.VMEM((TM, D_MODEL), _BF16)],
        compiler_params=pltpu.CompilerParams(
            dimension_semantics=("arbitrary",), vmem_limit_bytes=VMEM_LIMIT),
        name="merge",
    )(x2, g, osb, odf, wg, bg, wsb, wdf, wo)


def _ffn_kernel(h_ref, gf_ref, wg_ref, wu_ref, wd_ref, gl_ref, y_ref, acc_ref):
    h = h_ref[...]
    hn = _rms(h, gf_ref[...]).astype(_BF16)
    for n in range(D_FF // FF_CHUNK):
        sl = slice(n * FF_CHUNK, (n + 1) * FF_CHUNK)
        gate = jnp.dot(hn, wg_ref[:, sl], preferred_element_type=_F32)
        up = jnp.dot(hn, wu_ref[:, sl], preferred_element_type=_F32)
        act = (jax.nn.silu(gate) * up).astype(_BF16)
        down = jnp.dot(act, wd_ref[sl, :], preferred_element_type=_F32)
        if n == 0:
            acc_ref[...] = down
        else:
            acc_ref[...] += down
    y_ref[...] = _rms(h + acc_ref[...], gl_ref[...])


def _ffn(h1, gf, wg, wu, wd, gl):
    m = h1.shape[0]
    const = lambda shape: pl.BlockSpec(shape, lambda i: (0, 0), pipeline_mode=pl.Buffered(1))
    rows = lambda: pl.BlockSpec((TM, D_MODEL), lambda i: (i, 0))
    return pl.pallas_call(
        _ffn_kernel,
        out_shape=jax.ShapeDtypeStruct((m, D_MODEL), _F32),
        grid=(m // TM,),
        in_specs=[
            rows(), const((1, D_MODEL)),
            const((D_MODEL, D_FF)), const((D_MODEL, D_FF)), const((D_FF, D_MODEL)),
            const((1, D_MODEL)),
        ],
        out_specs=rows(),
        scratch_shapes=[pltpu.VMEM((TM, D_MODEL), _F32)],
        compiler_params=pltpu.CompilerParams(
            dimension_semantics=("arbitrary",), vmem_limit_bytes=VMEM_LIMIT),
        name="ffn",
    )(h1, gf, wg, wu, wd, gl)


def kernel(x, meta, norm_mix_g, w_in, w_gate, b_gate, lam_q1, lam_k1, lam_q2, lam_k2, subln_g,
           w_br_sb, w_br_df, w_out, norm_ffn_g, w_ffn_gate, w_ffn_up, w_ffn_down, norm_final_g):
    b, s, d = x.shape
    assert d == D_MODEL and s % SB_T == 0 and s % DF_T == 0 and (b * s) % TM == 0
    assert meta.shape == (N_META, D_MODEL) and w_in.shape[0] == 1

    bf = lambda w: w.astype(_BF16)
    x2 = x.reshape(b * s, d)
    g_mix = norm_mix_g[0][None, :]
    w_in_b = bf(w_in[0])

    proj = _in_proj(x2, g_mix, w_in_b, TM).reshape(b, s, IN_W)
    meta_proj = _in_proj(meta.astype(x.dtype), g_mix, w_in_b, N_META)
    meta_proj = jnp.pad(meta_proj, ((0, META_PAD - N_META), (0, 0)))

    slopes = jnp.exp2(-8.0 * (jnp.arange(HEADS) + 1) / HEADS).astype(_F32)
    lamvec = jnp.stack([lam_q1[0], lam_k1[0], lam_q2[0], lam_k2[0]]).astype(_F32)

    o_sb = _sb_attention(proj, meta_proj, b, s).reshape(b * s, d)
    o_df = _df_attention(proj, meta_proj, slopes, lamvec, subln_g[0][None, :], b, s).reshape(b * s, d)

    h1 = _merge(x2, g_mix, o_sb, o_df, bf(w_gate[0]), b_gate[0][None, :],
                bf(w_br_sb[0]), bf(w_br_df[0]), bf(w_out[0]))
    y = _ffn(h1, norm_ffn_g[0][None, :], bf(w_ffn_gate[0]), bf(w_ffn_up[0]), bf(w_ffn_down[0]),
             norm_final_g[None, :])
    return y.reshape(b, s, d)
```

```python
import functools
import math

import jax
import jax.numpy as jnp
from jax import lax
from jax.experimental import pallas as pl
from jax.experimental.pallas import tpu as pltpu

D_MODEL = 1024
N_META = 16
CHUNK = 64
HEADS = 8
HEAD_W = 128
DF_HALF = 64
IN_W = 6 * D_MODEL
D_FF = 2816
EPS = 1e-6
LAM_INIT = 0.8 - 0.6 * math.exp(-0.3 * 0)
LOG2E = 1.4426950408889634
SB_SCALE = HEAD_W ** -0.5
DF_SCALE2 = DF_HALF ** -0.5 * LOG2E

SBQ, SBK, SBV, DFQ, DFK, DFV = 0, 8, 16, 24, 32, 40

LANES = 128
META_PAD = LANES
SB_T = 256
SB_PAIR = 2
DF_T = 512
TM = 512
FF_CHUNK = 256
VMEM_LIMIT = 48 * 1024 * 1024

SB_DEAD = 110.0
DF_DEAD2 = 160.0
DF_BOUND_SLACK = 1.01

_BF16 = jnp.bfloat16
_F32 = jnp.float32
_NT = (((1,), (1,)), ((), ()))


def _rms(xf, g):
    ms = jnp.mean(xf * xf, axis=-1, keepdims=True)
    return xf * lax.rsqrt(ms + EPS) * g


def _max_row_norm(x):
    xf = x.astype(_F32)
    return jnp.max(jnp.sqrt(jnp.sum(xf * xf, axis=1, keepdims=True)))


def _inproj_kernel(x_ref, g_ref, w_ref, o_ref, *, n_chunk):
    xn = _rms(x_ref[...], g_ref[...]).astype(_BF16)
    for n in range(IN_W // n_chunk):
        sl = slice(n * n_chunk, (n + 1) * n_chunk)
        o_ref[:, sl] = jnp.dot(xn, w_ref[:, sl], preferred_element_type=_F32).astype(_BF16)


def _in_proj(x2, g, w, tm):
    m = x2.shape[0]
    return pl.pallas_call(
        functools.partial(_inproj_kernel, n_chunk=512),
        out_shape=jax.ShapeDtypeStruct((m, IN_W), _BF16),
        grid=(m // tm,),
        in_specs=[
            pl.BlockSpec((tm, D_MODEL), lambda i: (i, 0)),
            pl.BlockSpec((1, D_MODEL), lambda i: (0, 0)),
            pl.BlockSpec((D_MODEL, IN_W), lambda i: (0, 0), pipeline_mode=pl.Buffered(1)),
        ],
        out_specs=pl.BlockSpec((tm, IN_W), lambda i: (i, 0)),
        compiler_params=pltpu.CompilerParams(
            dimension_semantics=("arbitrary",), vmem_limit_bytes=VMEM_LIMIT),
        name="in_proj",
    )(x2, g, w)


def _sb_kernel(q_ref, k_ref, v_ref, km_ref, vm_ref, o_ref, acc_ref, c_ref):
    t_blk = SB_T
    i = pl.program_id(2)

    row = lax.broadcasted_iota(jnp.int32, (t_blk, t_blk), 0)
    col = lax.broadcasted_iota(jnp.int32, (t_blk, t_blk), 1)
    tri = (row > col).astype(_BF16)

    def head_block(hs, kt, vt, mask, tri_m, first):
        z = lax.dot_general(q_ref[:, hs], kt, _NT, preferred_element_type=_F32) * SB_SCALE
        t = jnp.log(1.0 + jnp.exp2(jnp.abs(z) * (-LOG2E)))
        sp = jnp.maximum(z, 0.0) + t
        lsig = jnp.minimum(z, 0.0) - t
        if mask is not None:
            sp = jnp.where(mask, sp, 0.0)
        suffix = jnp.dot(sp.astype(_BF16), tri_m, preferred_element_type=_F32)
        a = jnp.exp(lsig - suffix)
        if mask is not None:
            a = jnp.where(mask, a, 0.0)
        pv = jnp.dot(a.astype(_BF16), vt, preferred_element_type=_F32)
        csum = jnp.broadcast_to(jnp.sum(sp, axis=1, keepdims=True), (t_blk, LANES))
        if first:
            acc_ref[:, hs] = pv
            c_new = csum
        else:
            c_old = c_ref[:, hs]
            acc_ref[:, hs] += jnp.exp(-c_old) * pv
            c_new = c_old + csum
        c_ref[:, hs] = c_new
        return jnp.min(c_new)

    def block(k_src, v_src, mask, tri_m, first):
        cmin = None
        for hh in range(SB_PAIR):
            hs = slice(hh * HEAD_W, (hh + 1) * HEAD_W)
            cm = head_block(hs, k_src[:, hs], v_src[:, hs], mask, tri_m, first)
            cmin = cm if cmin is None else jnp.minimum(cmin, cm)
        return cmin

    d0 = pl.multiple_of(i * t_blk, t_blk)
    cmin0 = block(k_ref.at[pl.ds(d0, t_blk), :], v_ref.at[pl.ds(d0, t_blk), :], col < row, tri, True)

    def cond(carry):
        jj, cmin = carry
        return jnp.logical_and(jj < i, cmin < SB_DEAD)

    def body(carry):
        jj, _ = carry
        k0 = pl.multiple_of((i - 1 - jj) * t_blk, t_blk)
        cmin = block(k_ref.at[pl.ds(k0, t_blk), :], v_ref.at[pl.ds(k0, t_blk), :], None, tri, False)
        return jj + 1, cmin

    _, cmin = lax.while_loop(cond, body, (jnp.int32(0), cmin0))

    @pl.when(cmin < SB_DEAD)
    def _():
        mrow = lax.broadcasted_iota(jnp.int32, (META_PAD, META_PAD), 0)
        mcol = lax.broadcasted_iota(jnp.int32, (META_PAD, META_PAD), 1)
        valid = lax.broadcasted_iota(jnp.int32, (t_blk, META_PAD), 1) < N_META
        block(km_ref, vm_ref, valid, (mrow > mcol).astype(_BF16), False)

    o_ref[...] = acc_ref[...].astype(o_ref.dtype)


def _sb_attention(proj, meta_proj, b, s):
    t_blk = SB_T
    w = SB_PAIR * HEAD_W
    return pl.pallas_call(
        _sb_kernel,
        out_shape=jax.ShapeDtypeStruct((b, s, D_MODEL), _BF16),
        grid=(b, HEADS // SB_PAIR, s // t_blk),
        in_specs=[
            pl.BlockSpec((None, t_blk, w), lambda bi, h, i: (bi, i, SBQ // SB_PAIR + h)),
            pl.BlockSpec((None, s, w), lambda bi, h, i: (bi, 0, SBK // SB_PAIR + h)),
            pl.BlockSpec((None, s, w), lambda bi, h, i: (bi, 0, SBV // SB_PAIR + h)),
            pl.BlockSpec((META_PAD, w), lambda bi, h, i: (0, SBK // SB_PAIR + h)),
            pl.BlockSpec((META_PAD, w), lambda bi, h, i: (0, SBV // SB_PAIR + h)),
        ],
        out_specs=pl.BlockSpec((None, t_blk, w), lambda bi, h, i: (bi, i, h)),
        scratch_shapes=[
            pltpu.VMEM((t_blk, w), _F32),
            pltpu.VMEM((t_blk, w), _F32),
        ],
        compiler_params=pltpu.CompilerParams(
            dimension_semantics=("arbitrary", "arbitrary", "arbitrary"),
            vmem_limit_bytes=VMEM_LIMIT),
        name="sb_attn",
    )(proj, proj, proj, meta_proj, meta_proj)


def _df_kernel(slopes_ref, lam_ref, g_ref, q_ref, k_ref, v_ref, km_ref, vm_ref, o_ref,
               acc1, acc2, m1, m2, bias_ref, vext_ref, vmext_ref, kn_ref):
    t_blk = DF_T
    h = pl.program_id(1)
    i = pl.program_id(2)
    slope2 = slopes_ref[h] * LOG2E

    @pl.when(i == 0)
    def _():
        vext_ref[:, :HEAD_W] = v_ref[...]
        vext_ref[:, HEAD_W:] = jnp.ones((vext_ref.shape[0], LANES), _BF16)
        vmext_ref[:, :HEAD_W] = vm_ref[...]
        vmext_ref[:, HEAD_W:] = jnp.ones((META_PAD, LANES), _BF16)
        row = lax.broadcasted_iota(jnp.int32, (t_blk, 2 * t_blk), 0)
        col = lax.broadcasted_iota(jnp.int32, (t_blk, 2 * t_blk), 1)
        bias_ref[...] = -slope2 * (row - col).astype(_F32)
        kn_ref[0] = jnp.maximum(_max_row_norm(k_ref[...]), _max_row_norm(km_ref[...]))

    q = q_ref[...]
    lane = lax.broadcasted_iota(jnp.int32, (t_blk, HEAD_W), 1)
    zero = jnp.zeros_like(q)
    q_lo = jnp.where(lane < DF_HALF, q, zero)
    q_hi = jnp.where(lane >= DF_HALF, q, zero)
    score_bound = _max_row_norm(q) * kn_ref[0] * (DF_SCALE2 * DF_BOUND_SLACK)

    def update(y, vt, cb, acc, m, first):
        reps = y.shape[1] // LANES
        mx = jnp.broadcast_to(jnp.max(y, axis=1, keepdims=True), (t_blk, LANES)) + cb
        if first:
            m_new = mx
        else:
            m_old = m[...]
            m_new = jnp.maximum(m_old, mx)
        shift = m_new - cb
        p = jnp.exp2(y - jnp.concatenate([shift] * reps, axis=1))
        pv = jnp.dot(p.astype(_BF16), vt, preferred_element_type=_F32)
        if first:
            acc[...] = pv
        else:
            alpha = jnp.exp2(m_old - m_new)
            acc[...] = jnp.concatenate([alpha, alpha], axis=1) * acc[...] + pv
        m[...] = m_new
        return m_new

    def block(kt, vt, bias, mask, cb, first):
        y1 = lax.dot_general(q_lo, kt, _NT, preferred_element_type=_F32) * DF_SCALE2 + bias
        y2 = lax.dot_general(q_hi, kt, _NT, preferred_element_type=_F32) * DF_SCALE2 + bias
        if mask is not None:
            y1 = jnp.where(mask, y1, -jnp.inf)
            y2 = jnp.where(mask, y2, -jnp.inf)
        m1_new = update(y1, vt, cb, acc1, m1, first)
        m2_new = update(y2, vt, cb, acc2, m2, first)
        return jnp.min(jnp.minimum(m1_new, m2_new))

    def alive(dist, m_min):
        return score_bound - slope2 * dist.astype(_F32) - m_min > -DF_DEAD2

    d0 = pl.multiple_of(i * t_blk, t_blk)
    row = lax.broadcasted_iota(jnp.int32, (t_blk, t_blk), 0)
    col = lax.broadcasted_iota(jnp.int32, (t_blk, t_blk), 1)
    chunk_end = (row // CHUNK + 1) * CHUNK
    mmin0 = block(k_ref[pl.ds(d0, t_blk), :], vext_ref[pl.ds(d0, t_blk), :],
                  -jnp.abs(bias_ref[:, :t_blk]), col < chunk_end, 0.0, True)

    odd = i % 2

    def single(_):
        k0 = pl.multiple_of((i - 1) * t_blk, t_blk)
        return block(k_ref[pl.ds(k0, t_blk), :], vext_ref[pl.ds(k0, t_blk), :],
                     bias_ref[:, :t_blk], None, -slope2 * t_blk, False)

    mmin1 = lax.cond(jnp.logical_and(odd == 1, alive(jnp.int32(1), mmin0)),
                     single, lambda _: mmin0, 0)

    def cond(carry):
        jj, m_min = carry
        return jnp.logical_and(jj < i, alive(jj * t_blk + 1, m_min))

    def body(carry):
        jj, _ = carry
        k0 = pl.multiple_of((i - 2 - jj) * t_blk, t_blk)
        cb = -slope2 * ((jj + 2) * t_blk).astype(_F32)
        m_min = block(k_ref[pl.ds(k0, 2 * t_blk), :], vext_ref[pl.ds(k0, 2 * t_blk), :],
                      bias_ref[...], None, cb, False)
        return jj + 2, m_min

    _, mmin = lax.while_loop(cond, body, (odd, mmin1))

    @pl.when(alive(i * t_blk + 1, mmin))
    def _():
        cb_meta = -slope2 * (i * t_blk + N_META).astype(_F32)
        valid = lax.broadcasted_iota(jnp.int32, (t_blk, META_PAD), 1) < N_META
        block(km_ref[...], vmext_ref[...], bias_ref[:, :META_PAD], valid, cb_meta, False)

    lv = lam_ref[...]
    d1 = jnp.sum(lv[0:1, :] * lv[1:2, :], axis=1, keepdims=True)
    d2 = jnp.sum(lv[2:3, :] * lv[3:4, :], axis=1, keepdims=True)
    lam = jnp.exp(d1) - jnp.exp(d2) + LAM_INIT
    a1 = acc1[...]
    a2 = acc2[...]
    o = a1[:, :HEAD_W] * (1.0 / a1[:, HEAD_W:]) - lam * (a2[:, :HEAD_W] * (1.0 / a2[:, HEAD_W:]))
    o_ref[...] = (_rms(o, g_ref[...]) * (1.0 - LAM_INIT)).astype(o_ref.dtype)


def _df_attention(proj, meta_proj, slopes, lamvec, subln_g, b, s):
    t_blk = DF_T
    return pl.pallas_call(
        _df_kernel,
        out_shape=jax.ShapeDtypeStruct((b, s, D_MODEL), _BF16),
        grid=(b, HEADS, s // t_blk),
        in_specs=[
            pl.BlockSpec(memory_space=pltpu.SMEM),
            pl.BlockSpec((4, DF_HALF), lambda bi, h, i: (0, 0)),
            pl.BlockSpec((1, HEAD_W), lambda bi, h, i: (0, 0)),
            pl.BlockSpec((None, t_blk, HEAD_W), lambda bi, h, i: (bi, i, DFQ + h)),
            pl.BlockSpec((None, s, HEAD_W), lambda bi, h, i: (bi, 0, DFK + h)),
            pl.BlockSpec((None, s, HEAD_W), lambda bi, h, i: (bi, 0, DFV + h)),
            pl.BlockSpec((META_PAD, HEAD_W), lambda bi, h, i: (0, DFK + h)),
            pl.BlockSpec((META_PAD, HEAD_W), lambda bi, h, i: (0, DFV + h)),
        ],
        out_specs=pl.BlockSpec((None, t_blk, HEAD_W), lambda bi, h, i: (bi, i, h)),
        scratch_shapes=[
            pltpu.VMEM((t_blk, HEAD_W + LANES), _F32),
            pltpu.VMEM((t_blk, HEAD_W + LANES), _F32),
            pltpu.VMEM((t_blk, LANES), _F32),
            pltpu.VMEM((t_blk, LANES), _F32),
            pltpu.VMEM((t_blk, 2 * t_blk), _F32),
            pltpu.VMEM((s, HEAD_W + LANES), _BF16),
            pltpu.VMEM((META_PAD, HEAD_W + LANES), _BF16),
            pltpu.SMEM((1,), _F32),
        ],
        compiler_params=pltpu.CompilerParams(
            dimension_semantics=("arbitrary", "arbitrary", "arbitrary"),
            vmem_limit_bytes=VMEM_LIMIT),
        name="df_attn",
    )(slopes, lamvec, subln_g, proj, proj, proj, meta_proj, meta_proj)


def _merge_kernel(x_ref, g_ref, osb_ref, odf_ref, wg_ref, bg_ref, wsb_ref, wdf_ref, wo_ref,
                  h_ref, merged_ref, *, n_chunk):
    x = x_ref[...]
    xn = _rms(x, g_ref[...]).astype(_BF16)
    osb = osb_ref[...]
    odf = odf_ref[...]
    for n in range(D_MODEL // n_chunk):
        sa = slice(n * n_chunk, (n + 1) * n_chunk)
        sb = slice(D_MODEL + n * n_chunk, D_MODEL + (n + 1) * n_chunk)
        ga = jax.nn.sigmoid(jnp.dot(xn, wg_ref[:, sa], preferred_element_type=_F32) + bg_ref[:, sa])
        gb = jax.nn.sigmoid(jnp.dot(xn, wg_ref[:, sb], preferred_element_type=_F32) + bg_ref[:, sb])
        ysb = jnp.dot(osb, wsb_ref[:, sa], preferred_element_type=_F32)
        ydf = jnp.dot(odf, wdf_ref[:, sa], preferred_element_type=_F32)
        merged_ref[:, sa] = (ga * ysb + gb * ydf).astype(_BF16)
    h_ref[...] = x + jnp.dot(merged_ref[...], wo_ref[...], preferred_element_type=_F32)


def _merge(x2, g, osb, odf, wg, bg, wsb, wdf, wo):
    m = x2.shape[0]
    const = lambda shape: pl.BlockSpec(shape, lambda i: (0, 0), pipeline_mode=pl.Buffered(1))
    rows = lambda: pl.BlockSpec((TM, D_MODEL), lambda i: (i, 0))
    return pl.pallas_call(
        functools.partial(_merge_kernel, n_chunk=512),
        out_shape=jax.ShapeDtypeStruct((m, D_MODEL), _F32),
        grid=(m // TM,),
        in_specs=[
            rows(), const((1, D_MODEL)), rows(), rows(),
            const((D_MODEL, 2 * D_MODEL)), const((1, 2 * D_MODEL)),
            const((D_MODEL, D_MODEL)), const((D_MODEL, D_MODEL)), const((D_MODEL, D_MODEL)),
        ],
        out_specs=rows(),
        scratch_shapes=[pltpu.VMEM((TM, D_MODEL), _BF16)],
        compiler_params=pltpu.CompilerParams(
            dimension_semantics=("arbitrary",), vmem_limit_bytes=VMEM_LIMIT),
        name="merge",
    )(x2, g, osb, odf, wg, bg, wsb, wdf, wo)


def _ffn_kernel(h_ref, gf_ref, wg_ref, wu_ref, wd_ref, gl_ref, y_ref, acc_ref):
    h = h_ref[...]
    hn = _rms(h, gf_ref[...]).astype(_BF16)
    for n in range(D_FF // FF_CHUNK):
        sl = slice(n * FF_CHUNK, (n + 1) * FF_CHUNK)
        gate = jnp.dot(hn, wg_ref[:, sl], preferred_element_type=_F32)
        up = jnp.dot(hn, wu_ref[:, sl], preferred_element_type=_F32)
        act = (jax.nn.silu(gate) * up).astype(_BF16)
        down = jnp.dot(act, wd_ref[sl, :], preferred_element_type=_F32)
        if n == 0:
            acc_ref[...] = down
        else:
            acc_ref[...] += down
    y_ref[...] = _rms(h + acc_ref[...], gl_ref[...])


def _ffn(h1, gf, wg, wu, wd, gl):
    m = h1.shape[0]
    const = lambda shape: pl.BlockSpec(shape, lambda i: (0, 0), pipeline_mode=pl.Buffered(1))
    rows = lambda: pl.BlockSpec((TM, D_MODEL), lambda i: (i, 0))
    return pl.pallas_call(
        _ffn_kernel,
        out_shape=jax.ShapeDtypeStruct((m, D_MODEL), _F32),
        grid=(m // TM,),
        in_specs=[
            rows(), const((1, D_MODEL)),
            const((D_MODEL, D_FF)), const((D_MODEL, D_FF)), const((D_FF, D_MODEL)),
            const((1, D_MODEL)),
        ],
        out_specs=rows(),
        scratch_shapes=[pltpu.VMEM((TM, D_MODEL), _F32)],
        compiler_params=pltpu.CompilerParams(
            dimension_semantics=("arbitrary",), vmem_limit_bytes=VMEM_LIMIT),
        name="ffn",
    )(h1, gf, wg, wu, wd, gl)


def kernel(x, meta, norm_mix_g, w_in, w_gate, b_gate, lam_q1, lam_k1, lam_q2, lam_k2, subln_g,
           w_br_sb, w_br_df, w_out, norm_ffn_g, w_ffn_gate, w_ffn_up, w_ffn_down, norm_final_g):
    b, s, d = x.shape
    assert d == D_MODEL and s % SB_T == 0 and s % DF_T == 0 and (b * s) % TM == 0
    assert meta.shape == (N_META, D_MODEL) and w_in.shape[0] == 1

    bf = lambda w: w.astype(_BF16)
    x2 = x.reshape(b * s, d)
    g_mix = norm_mix_g[0][None, :]
    w_in_b = bf(w_in[0])

    proj = _in_proj(x2, g_mix, w_in_b, TM).reshape(b, s, IN_W)
    meta_proj = _in_proj(meta.astype(x.dtype), g_mix, w_in_b, N_META)
    meta_proj = jnp.pad(meta_proj, ((0, META_PAD - N_META), (0, 0)))

    slopes = jnp.exp2(-8.0 * (jnp.arange(HEADS) + 1) / HEADS).astype(_F32)
    lamvec = jnp.stack([lam_q1[0], lam_k1[0], lam_q2[0], lam_k2[0]]).astype(_F32)

    o_sb = _sb_attention(proj, meta_proj, b, s).reshape(b * s, d)
    o_df = _df_attention(proj, meta_proj, slopes, lamvec, subln_g[0][None, :], b, s).reshape(b * s, d)

    h1 = _merge(x2, g_mix, o_sb, o_df, bf(w_gate[0]), b_gate[0][None, :],
                bf(w_br_sb[0]), bf(w_br_df[0]), bf(w_out[0]))
    y = _ffn(h1, norm_ffn_g[0][None, :], bf(w_ffn_gate[0]), bf(w_ffn_up[0]), bf(w_ffn_down[0]),
             norm_final_g[None, :])
    return y.reshape(b, s, d)
```

```python
import functools
import math

import jax
import jax.numpy as jnp
from jax import lax
from jax.experimental import pallas as pl
from jax.experimental.pallas import tpu as pltpu

D_MODEL = 1024
N_META = 16
CHUNK = 64
HEADS = 8
HEAD_W = 128
DF_HALF = 64
IN_W = 6 * D_MODEL
D_FF = 2816
EPS = 1e-6
LAM_INIT = 0.8 - 0.6 * math.exp(-0.3 * 0)
LOG2E = 1.4426950408889634
SB_SCALE = HEAD_W ** -0.5
DF_SCALE = DF_HALF ** -0.5

SBQ, SBK, SBV, DFQ, DFK, DFV = 0, 8, 16, 24, 32, 40

LANES = 128
META_PAD = LANES
SB_T = 256
SB_PAIR = 4
DF_T = 512
TM = 512
FF_CHUNK = 256
VMEM_LIMIT = 48 * 1024 * 1024

SB_DEAD = 110.0
DF_DEAD = 112.0
DF_BOUND_SLACK = 1.01
KPOS_RADIX = 32
DF_ROW_SPLIT = 1

_BF16 = jnp.bfloat16
_F32 = jnp.float32
_NT = (((1,), (1,)), ((), ()))


def _rms(xf, g):
    ms = jnp.mean(xf * xf, axis=-1, keepdims=True)
    return xf * lax.rsqrt(ms + EPS) * g


def _max_row_norm(x):
    xf = x.astype(_F32)
    return jnp.max(jnp.sqrt(jnp.sum(xf * xf, axis=1, keepdims=True)))


def _inproj_kernel(x_ref, g_ref, w_ref, o_ref, *, n_chunk):
    xn = _rms(x_ref[...], g_ref[...]).astype(_BF16)
    for n in range(IN_W // n_chunk):
        sl = slice(n * n_chunk, (n + 1) * n_chunk)
        o_ref[:, sl] = jnp.dot(xn, w_ref[:, sl], preferred_element_type=_F32).astype(_BF16)


def _in_proj(x2, g, w, tm):
    m = x2.shape[0]
    return pl.pallas_call(
        functools.partial(_inproj_kernel, n_chunk=512),
        out_shape=jax.ShapeDtypeStruct((m, IN_W), _BF16),
        grid=(m // tm,),
        in_specs=[
            pl.BlockSpec((tm, D_MODEL), lambda i: (i, 0)),
            pl.BlockSpec((1, D_MODEL), lambda i: (0, 0)),
            pl.BlockSpec((D_MODEL, IN_W), lambda i: (0, 0), pipeline_mode=pl.Buffered(1)),
        ],
        out_specs=pl.BlockSpec((tm, IN_W), lambda i: (i, 0)),
        compiler_params=pltpu.CompilerParams(
            dimension_semantics=("arbitrary",), vmem_limit_bytes=VMEM_LIMIT),
        name="in_proj",
    )(x2, g, w)


def _sb_kernel(q_ref, k_ref, v_ref, km_ref, vm_ref, o_ref, acc_ref, c_ref):
    t_blk = SB_T
    i = pl.program_id(2)

    row = lax.broadcasted_iota(jnp.int32, (t_blk, t_blk), 0)
    col = lax.broadcasted_iota(jnp.int32, (t_blk, t_blk), 1)
    tri = (row > col).astype(_BF16)

    def head_block(hs, kt, vt, mask, tri_m, first):
        z = lax.dot_general(q_ref[:, hs], kt, _NT, preferred_element_type=_F32) * SB_SCALE
        t = jnp.log(1.0 + jnp.exp2(jnp.abs(z) * (-LOG2E)))
        sp = jnp.maximum(z, 0.0) + t
        lsig = jnp.minimum(z, 0.0) - t
        if mask is not None:
            sp = jnp.where(mask, sp, 0.0)
        suffix = jnp.dot(sp.astype(_BF16), tri_m, preferred_element_type=_F32)
        a = jnp.exp(lsig - suffix)
        if mask is not None:
            a = jnp.where(mask, a, 0.0)
        pv = jnp.dot(a.astype(_BF16), vt, preferred_element_type=_F32)
        csum = jnp.broadcast_to(jnp.sum(sp, axis=1, keepdims=True), (t_blk, LANES))
        if first:
            acc_ref[:, hs] = pv
            c_new = csum
        else:
            c_old = c_ref[:, hs]
            acc_ref[:, hs] += jnp.exp(-c_old) * pv
            c_new = c_old + csum
        c_ref[:, hs] = c_new
        return jnp.min(c_new)

    def block(k_src, v_src, mask, tri_m, first):
        cmin = None
        for hh in range(SB_PAIR):
            hs = slice(hh * HEAD_W, (hh + 1) * HEAD_W)
            cm = head_block(hs, k_src[:, hs], v_src[:, hs], mask, tri_m, first)
            cmin = cm if cmin is None else jnp.minimum(cmin, cm)
        return cmin

    d0 = pl.multiple_of(i * t_blk, t_blk)
    cmin0 = block(k_ref.at[pl.ds(d0, t_blk), :], v_ref.at[pl.ds(d0, t_blk), :], col < row, tri, True)

    def cond(carry):
        jj, cmin = carry
        return jnp.logical_and(jj < i, cmin < SB_DEAD)

    def body(carry):
        jj, _ = carry
        k0 = pl.multiple_of((i - 1 - jj) * t_blk, t_blk)
        cmin = block(k_ref.at[pl.ds(k0, t_blk), :], v_ref.at[pl.ds(k0, t_blk), :], None, tri, False)
        return jj + 1, cmin

    _, cmin = lax.while_loop(cond, body, (jnp.int32(0), cmin0))

    @pl.when(cmin < SB_DEAD)
    def _():
        mrow = lax.broadcasted_iota(jnp.int32, (META_PAD, META_PAD), 0)
        mcol = lax.broadcasted_iota(jnp.int32, (META_PAD, META_PAD), 1)
        valid = lax.broadcasted_iota(jnp.int32, (t_blk, META_PAD), 1) < N_META
        block(km_ref, vm_ref, valid, (mrow > mcol).astype(_BF16), False)

    o_ref[...] = acc_ref[...].astype(o_ref.dtype)


def _sb_attention(proj, meta_proj, b, s):
    t_blk = SB_T
    w = SB_PAIR * HEAD_W
    return pl.pallas_call(
        _sb_kernel,
        out_shape=jax.ShapeDtypeStruct((b, s, D_MODEL), _BF16),
        grid=(b, HEADS // SB_PAIR, s // t_blk),
        in_specs=[
            pl.BlockSpec((None, t_blk, w), lambda bi, h, i: (bi, i, SBQ // SB_PAIR + h)),
            pl.BlockSpec((None, s, w), lambda bi, h, i: (bi, 0, SBK // SB_PAIR + h)),
            pl.BlockSpec((None, s, w), lambda bi, h, i: (bi, 0, SBV // SB_PAIR + h)),
            pl.BlockSpec((META_PAD, w), lambda bi, h, i: (0, SBK // SB_PAIR + h)),
            pl.BlockSpec((META_PAD, w), lambda bi, h, i: (0, SBV // SB_PAIR + h)),
        ],
        out_specs=pl.BlockSpec((None, t_blk, w), lambda bi, h, i: (bi, i, h)),
        scratch_shapes=[
            pltpu.VMEM((t_blk, w), _F32),
            pltpu.VMEM((t_blk, w), _F32),
        ],
        compiler_params=pltpu.CompilerParams(
            dimension_semantics=("arbitrary", "arbitrary", "arbitrary"),
            vmem_limit_bytes=VMEM_LIMIT),
        name="sb_attn",
    )(proj, proj, proj, meta_proj, meta_proj)


def _df_kernel(slopes_ref, lam_ref, g_ref, q_ref, k_ref, v_ref, km_ref, vm_ref, o_ref,
               acc1, acc2, m1, m2, bias_ref, vext_ref, vmext_ref, kpos_ref, kn_ref):
    t_blk = DF_T
    h = pl.program_id(1)
    i = pl.program_id(2)
    slope = slopes_ref[h]

    @pl.when(i == 0)
    def _():
        vext_ref[:, :HEAD_W] = v_ref[...]
        vext_ref[:, HEAD_W:] = jnp.ones((vext_ref.shape[0], LANES), _BF16)
        vmext_ref[:, :HEAD_W] = vm_ref[...]
        vmext_ref[:, HEAD_W:] = jnp.ones((META_PAD, LANES), _BF16)
        row = lax.broadcasted_iota(jnp.int32, (t_blk, t_blk), 0)
        col = lax.broadcasted_iota(jnp.int32, (t_blk, t_blk), 1)
        bias_ref[...] = slope * (row - jnp.abs(row - col)).astype(_F32)
        c = lax.broadcasted_iota(jnp.int32, (2 * t_blk, LANES), 0)
        ln = lax.broadcasted_iota(jnp.int32, (2 * t_blk, LANES), 1)
        kpos = jnp.where(ln == 0, c // KPOS_RADIX, jnp.where(ln == 1, c % KPOS_RADIX, 0))
        kpos_ref[...] = kpos.astype(_F32).astype(_BF16)
        kn_ref[0] = jnp.maximum(_max_row_norm(k_ref[...]), _max_row_norm(km_ref[...]))

    q = q_ref[...]
    qs = q * DF_SCALE
    lane = lax.broadcasted_iota(jnp.int32, (t_blk, HEAD_W), 1)
    zero = jnp.zeros_like(q)
    q_lo = jnp.where(lane < DF_HALF, qs, zero)
    q_hi = jnp.where(lane >= DF_HALF, qs, zero)
    qx = jnp.where(lane == 0, slope * KPOS_RADIX, jnp.where(lane == 1, slope, 0.0)).astype(_BF16)
    q1 = jnp.concatenate([q_lo, qx], axis=1)
    q2 = jnp.concatenate([q_hi, qx], axis=1)
    score_bound = _max_row_norm(q) * kn_ref[0] * (DF_SCALE * DF_BOUND_SLACK)

    def update(y, vt, cb, acc, m, rs, first):
        rows, reps = y.shape[0], y.shape[1] // LANES
        mx = jnp.broadcast_to(jnp.max(y, axis=1, keepdims=True), (rows, LANES)) + cb
        if first:
            m_new = mx
        else:
            m_old = m[rs, :]
            m_new = jnp.maximum(m_old, mx)
        shift = m_new - cb
        p = jnp.exp(y - jnp.concatenate([shift] * reps, axis=1))
        pv = jnp.dot(p.astype(_BF16), vt, preferred_element_type=_F32)
        if first:
            acc[rs, :] = pv
        else:
            alpha = jnp.exp(m_old - m_new)
            acc[rs, :] = jnp.concatenate([alpha, alpha], axis=1) * acc[rs, :] + pv
        m[rs, :] = m_new
        return m_new

    def finish(m1_new, m2_new, r0):
        t = lax.broadcasted_iota(jnp.int32, m1_new.shape, 0) + r0
        return jnp.min(jnp.minimum(m1_new, m2_new) - slope * t.astype(_F32))

    all_rows = slice(0, t_blk)

    def older_tile(kt, vt, width, n_valid, cb):
        k_aug = jnp.concatenate([kt, kpos_ref[:width, :]], axis=1)
        rows = t_blk // DF_ROW_SPLIT
        groups = [slice(r * rows, (r + 1) * rows) for r in range(DF_ROW_SPLIT)]

        def scores(rs):
            ys = []
            for qa in (q1, q2):
                y = lax.dot_general(qa[rs, :], k_aug, _NT, preferred_element_type=_F32)
                if n_valid is not None:
                    valid = lax.broadcasted_iota(jnp.int32, (rows, width), 1) < n_valid
                    y = jnp.where(valid, y, -jnp.inf)
                ys.append(y)
            return ys

        def softmax_pv(r, ys):
            rs = groups[r]
            return finish(update(ys[0], vt, cb, acc1, m1, rs, False),
                          update(ys[1], vt, cb, acc2, m2, rs, False), r * rows)

        m_min = None
        pending = scores(groups[0])
        for r in range(DF_ROW_SPLIT):
            nxt = scores(groups[r + 1]) if r + 1 < DF_ROW_SPLIT else None
            mm = softmax_pv(r, pending)
            m_min = mm if m_min is None else jnp.minimum(m_min, mm)
            pending = nxt
        return m_min

    def alive(dist, m_min):
        return score_bound - slope * dist.astype(_F32) - m_min > -DF_DEAD

    d0 = pl.multiple_of(i * t_blk, t_blk)
    row = lax.broadcasted_iota(jnp.int32, (t_blk, t_blk), 0)
    col = lax.broadcasted_iota(jnp.int32, (t_blk, t_blk), 1)
    visible = col < (row // CHUNK + 1) * CHUNK
    kd = k_ref[pl.ds(d0, t_blk), :]
    vd = vext_ref[pl.ds(d0, t_blk), :]
    yd1 = lax.dot_general(q_lo, kd, _NT, preferred_element_type=_F32) + bias_ref[...]
    yd2 = lax.dot_general(q_hi, kd, _NT, preferred_element_type=_F32) + bias_ref[...]
    mmin0 = finish(update(jnp.where(visible, yd1, -jnp.inf), vd, 0.0, acc1, m1, all_rows, True),
                   update(jnp.where(visible, yd2, -jnp.inf), vd, 0.0, acc2, m2, all_rows, True), 0)

    odd = i % 2

    def single(_):
        k0 = pl.multiple_of((i - 1) * t_blk, t_blk)
        return older_tile(k_ref[pl.ds(k0, t_blk), :], vext_ref[pl.ds(k0, t_blk), :],
                          t_blk, None, -slope * t_blk)

    mmin1 = lax.cond(jnp.logical_and(odd == 1, alive(jnp.int32(1), mmin0)),
                     single, lambda _: mmin0, 0)

    def cond(carry):
        jj, m_min = carry
        return jnp.logical_and(jj < i, alive(jj * t_blk + 1, m_min))

    def body(carry):
        jj, _ = carry
        k0 = pl.multiple_of((i - 2 - jj) * t_blk, t_blk)
        cb = -slope * ((jj + 2) * t_blk).astype(_F32)
        m_min = older_tile(k_ref[pl.ds(k0, 2 * t_blk), :], vext_ref[pl.ds(k0, 2 * t_blk), :],
                           2 * t_blk, None, cb)
        return jj + 2, m_min

    _, mmin = lax.while_loop(cond, body, (odd, mmin1))

    @pl.when(alive(i * t_blk + 1, mmin))
    def _():
        cb_meta = -slope * (i * t_blk + N_META).astype(_F32)
        older_tile(km_ref[...], vmext_ref[...], META_PAD, N_META, cb_meta)

    lv = lam_ref[...]
    d1 = jnp.sum(lv[0:1, :] * lv[1:2, :], axis=1, keepdims=True)
    d2 = jnp.sum(lv[2:3, :] * lv[3:4, :], axis=1, keepdims=True)
    lam = jnp.exp(d1) - jnp.exp(d2) + LAM_INIT
    a1 = acc1[...]
    a2 = acc2[...]
    o = a1[:, :HEAD_W] * (1.0 / a1[:, HEAD_W:]) - lam * (a2[:, :HEAD_W] * (1.0 / a2[:, HEAD_W:]))
    o_ref[...] = (_rms(o, g_ref[...]) * (1.0 - LAM_INIT)).astype(o_ref.dtype)


def _df_attention(proj, meta_proj, slopes, lamvec, subln_g, b, s):
    t_blk = DF_T
    return pl.pallas_call(
        _df_kernel,
        out_shape=jax.ShapeDtypeStruct((b, s, D_MODEL), _BF16),
        grid=(b, HEADS, s // t_blk),
        in_specs=[
            pl.BlockSpec(memory_space=pltpu.SMEM),
            pl.BlockSpec((4, DF_HALF), lambda bi, h, i: (0, 0)),
            pl.BlockSpec((1, HEAD_W), lambda bi, h, i: (0, 0)),
            pl.BlockSpec((None, t_blk, HEAD_W), lambda bi, h, i: (bi, i, DFQ + h)),
            pl.BlockSpec((None, s, HEAD_W), lambda bi, h, i: (bi, 0, DFK + h)),
            pl.BlockSpec((None, s, HEAD_W), lambda bi, h, i: (bi, 0, DFV + h)),
            pl.BlockSpec((META_PAD, HEAD_W), lambda bi, h, i: (0, DFK + h)),
            pl.BlockSpec((META_PAD, HEAD_W), lambda bi, h, i: (0, DFV + h)),
        ],
        out_specs=pl.BlockSpec((None, t_blk, HEAD_W), lambda bi, h, i: (bi, i, h)),
        scratch_shapes=[
            pltpu.VMEM((t_blk, HEAD_W + LANES), _F32),
            pltpu.VMEM((t_blk, HEAD_W + LANES), _F32),
            pltpu.VMEM((t_blk, LANES), _F32),
            pltpu.VMEM((t_blk, LANES), _F32),
            pltpu.VMEM((t_blk, t_blk), _F32),
            pltpu.VMEM((s, HEAD_W + LANES), _BF16),
            pltpu.VMEM((META_PAD, HEAD_W + LANES), _BF16),
            pltpu.VMEM((2 * t_blk, LANES), _BF16),
            pltpu.SMEM((1,), _F32),
        ],
        compiler_params=pltpu.CompilerParams(
            dimension_semantics=("arbitrary", "arbitrary", "arbitrary"),
            vmem_limit_bytes=VMEM_LIMIT),
        name="df_attn",
    )(slopes, lamvec, subln_g, proj, proj, proj, meta_proj, meta_proj)


def _merge_kernel(x_ref, g_ref, osb_ref, odf_ref, wg_ref, bg_ref, wsb_ref, wdf_ref, wo_ref,
                  h_ref, merged_ref, *, n_chunk):
    x = x_ref[...]
    xn = _rms(x, g_ref[...]).astype(_BF16)
    osb = osb_ref[...]
    odf = odf_ref[...]
    for n in range(D_MODEL // n_chunk):
        sa = slice(n * n_chunk, (n + 1) * n_chunk)
        sb = slice(D_MODEL + n * n_chunk, D_MODEL + (n + 1) * n_chunk)
        ga = jax.nn.sigmoid(jnp.dot(xn, wg_ref[:, sa], preferred_element_type=_F32) + bg_ref[:, sa])
        gb = jax.nn.sigmoid(jnp.dot(xn, wg_ref[:, sb], preferred_element_type=_F32) + bg_ref[:, sb])
        ysb = jnp.dot(osb, wsb_ref[:, sa], preferred_element_type=_F32)
        ydf = jnp.dot(odf, wdf_ref[:, sa], preferred_element_type=_F32)
        merged_ref[:, sa] = (ga * ysb + gb * ydf).astype(_BF16)
    h_ref[...] = x + jnp.dot(merged_ref[...], wo_ref[...], preferred_element_type=_F32)


def _merge(x2, g, osb, odf, wg, bg, wsb, wdf, wo):
    m = x2.shape[0]
    const = lambda shape: pl.BlockSpec(shape, lambda i: (0, 0), pipeline_mode=pl.Buffered(1))
    rows = lambda: pl.BlockSpec((TM, D_MODEL), lambda i: (i, 0))
    return pl.pallas_call(
        functools.partial(_merge_kernel, n_chunk=512),
        out_shape=jax.ShapeDtypeStruct((m, D_MODEL), _F32),
        grid=(m // TM,),
        in_specs=[
            rows(), const((1, D_MODEL)), rows(), rows(),
            const((D_MODEL, 2 * D_MODEL)), const((1, 2 * D_MODEL)),
            const((D_MODEL, D_MODEL)), const((D_MODEL, D_MODEL)), const((D_MODEL, D_MODEL)),
        ],
        out_specs=rows(),
        scratch_shapes=[pltpu.VMEM((TM, D_MODEL), _BF16)],
        compiler_params=pltpu.CompilerParams(
            dimension_semantics=("arbitrary",), vmem_limit_bytes=VMEM_LIMIT),
        name="merge",
    )(x2, g, osb, odf, wg, bg, wsb, wdf, wo)


def _ffn_kernel(h_ref, gf_ref, wg_ref, wu_ref, wd_ref, gl_ref, y_ref, acc_ref):
    h = h_ref[...]
    hn = _rms(h, gf_ref[...]).astype(_BF16)
    for n in range(D_FF // FF_CHUNK):
        sl = slice(n * FF_CHUNK, (n + 1) * FF_CHUNK)
        gate = jnp.dot(hn, wg_ref[:, sl], preferred_element_type=_F32)
        up = jnp.dot(hn, wu_ref[:, sl], preferred_element_type=_F32)
        act = (jax.nn.silu(gate) * up).astype(_BF16)
        down = jnp.dot(act, wd_ref[sl, :], preferred_element_type=_F32)
        if n == 0:
            acc_ref[...] = down
        else:
            acc_ref[...] += down
    y_ref[...] = _rms(h + acc_ref[...], gl_ref[...])


def _ffn(h1, gf, wg, wu, wd, gl):
    m = h1.shape[0]
    const = lambda shape: pl.BlockSpec(shape, lambda i: (0, 0), pipeline_mode=pl.Buffered(1))
    rows = lambda: pl.BlockSpec((TM, D_MODEL), lambda i: (i, 0))
    return pl.pallas_call(
        _ffn_kernel,
        out_shape=jax.ShapeDtypeStruct((m, D_MODEL), _F32),
        grid=(m // TM,),
        in_specs=[
            rows(), const((1, D_MODEL)),
            const((D_MODEL, D_FF)), const((D_MODEL, D_FF)), const((D_FF, D_MODEL)),
            const((1, D_MODEL)),
        ],
        out_specs=rows(),
        scratch_shapes=[pltpu.VMEM((TM, D_MODEL), _F32)],
        compiler_params=pltpu.CompilerParams(
            dimension_semantics=("arbitrary",), vmem_limit_bytes=VMEM_LIMIT),
        name="ffn",
    )(h1, gf, wg, wu, wd, gl)


def kernel(x, meta, norm_mix_g, w_in, w_gate, b_gate, lam_q1, lam_k1, lam_q2, lam_k2, subln_g,
           w_br_sb, w_br_df, w_out, norm_ffn_g, w_ffn_gate, w_ffn_up, w_ffn_down, norm_final_g):
    b, s, d = x.shape
    assert d == D_MODEL and s % SB_T == 0 and s % DF_T == 0 and (b * s) % TM == 0
    assert meta.shape == (N_META, D_MODEL) and w_in.shape[0] == 1

    bf = lambda w: w.astype(_BF16)
    x2 = x.reshape(b * s, d)
    g_mix = norm_mix_g[0][None, :]
    w_in_b = bf(w_in[0])

    proj = _in_proj(x2, g_mix, w_in_b, TM).reshape(b, s, IN_W)
    meta_proj = _in_proj(meta.astype(x.dtype), g_mix, w_in_b, N_META)
    meta_proj = jnp.pad(meta_proj, ((0, META_PAD - N_META), (0, 0)))

    slopes = jnp.exp2(-8.0 * (jnp.arange(HEADS) + 1) / HEADS).astype(_F32)
    lamvec = jnp.stack([lam_q1[0], lam_k1[0], lam_q2[0], lam_k2[0]]).astype(_F32)

    o_sb = _sb_attention(proj, meta_proj, b, s).reshape(b * s, d)
    o_df = _df_attention(proj, meta_proj, slopes, lamvec, subln_g[0][None, :], b, s).reshape(b * s, d)

    h1 = _merge(x2, g_mix, o_sb, o_df, bf(w_gate[0]), b_gate[0][None, :],
                bf(w_br_sb[0]), bf(w_br_df[0]), bf(w_out[0]))
    y = _ffn(h1, norm_ffn_g[0][None, :], bf(w_ffn_gate[0]), bf(w_ffn_up[0]), bf(w_ffn_down[0]),
             norm_final_g[None, :])
    return y.reshape(b, s, d)
```

```python
import functools
import math

import jax
import jax.numpy as jnp
from jax import lax
from jax.experimental import pallas as pl
from jax.experimental.pallas import tpu as pltpu

D_MODEL = 1024
N_META = 16
CHUNK = 64
HEADS = 8
HEAD_W = 128
DF_HALF = 64
IN_W = 6 * D_MODEL
D_FF = 2816
EPS = 1e-6
LAM_INIT = 0.8 - 0.6 * math.exp(-0.3 * 0)
LOG2E = 1.4426950408889634
SB_SCALE = HEAD_W ** -0.5
DF_SCALE = DF_HALF ** -0.5

SBQ, SBK, SBV, DFQ, DFK, DFV = 0, 8, 16, 24, 32, 40

LANES = 128
META_PAD = LANES
SB_T = 256
SB_PAIR = 4
DF_T = 512
TM = 512
FF_CHUNK = 256
VMEM_LIMIT = 48 * 1024 * 1024

SB_DEAD = 110.0
DF_DEAD = 112.0
DF_BOUND_SLACK = 1.01
KPOS_RADIX = 32
DF_SAFE = 60.0

_BF16 = jnp.bfloat16
_F32 = jnp.float32
_NT = (((1,), (1,)), ((), ()))


def _rms(xf, g):
    ms = jnp.mean(xf * xf, axis=-1, keepdims=True)
    return xf * lax.rsqrt(ms + EPS) * g


def _max_row_norm(x):
    xf = x.astype(_F32)
    return jnp.max(jnp.sqrt(jnp.sum(xf * xf, axis=1, keepdims=True)))


def _inproj_kernel(x_ref, g_ref, w_ref, o_ref, *, n_chunk):
    xn = _rms(x_ref[...], g_ref[...]).astype(_BF16)
    for n in range(IN_W // n_chunk):
        sl = slice(n * n_chunk, (n + 1) * n_chunk)
        o_ref[:, sl] = jnp.dot(xn, w_ref[:, sl], preferred_element_type=_F32).astype(_BF16)


def _in_proj(x2, g, w, tm):
    m = x2.shape[0]
    return pl.pallas_call(
        functools.partial(_inproj_kernel, n_chunk=512),
        out_shape=jax.ShapeDtypeStruct((m, IN_W), _BF16),
        grid=(m // tm,),
        in_specs=[
            pl.BlockSpec((tm, D_MODEL), lambda i: (i, 0)),
            pl.BlockSpec((1, D_MODEL), lambda i: (0, 0)),
            pl.BlockSpec((D_MODEL, IN_W), lambda i: (0, 0), pipeline_mode=pl.Buffered(1)),
        ],
        out_specs=pl.BlockSpec((tm, IN_W), lambda i: (i, 0)),
        compiler_params=pltpu.CompilerParams(
            dimension_semantics=("arbitrary",), vmem_limit_bytes=VMEM_LIMIT),
        name="in_proj",
    )(x2, g, w)


def _sb_kernel(q_ref, k_ref, v_ref, km_ref, vm_ref, o_ref, acc_ref, c_ref):
    t_blk = SB_T
    i = pl.program_id(2)

    row = lax.broadcasted_iota(jnp.int32, (t_blk, t_blk), 0)
    col = lax.broadcasted_iota(jnp.int32, (t_blk, t_blk), 1)
    tri = (row > col).astype(_BF16)

    def head_block(hs, kt, vt, mask, tri_m, first):
        z = lax.dot_general(q_ref[:, hs], kt, _NT, preferred_element_type=_F32) * SB_SCALE
        t = jnp.log(1.0 + jnp.exp2(jnp.abs(z) * (-LOG2E)))
        sp = jnp.maximum(z, 0.0) + t
        lsig = jnp.minimum(z, 0.0) - t
        if mask is not None:
            sp = jnp.where(mask, sp, 0.0)
        suffix = jnp.dot(sp.astype(_BF16), tri_m, preferred_element_type=_F32)
        a = jnp.exp(lsig - suffix)
        if mask is not None:
            a = jnp.where(mask, a, 0.0)
        pv = jnp.dot(a.astype(_BF16), vt, preferred_element_type=_F32)
        csum = jnp.broadcast_to(jnp.sum(sp, axis=1, keepdims=True), (t_blk, LANES))
        if first:
            acc_ref[:, hs] = pv
            c_new = csum
        else:
            c_old = c_ref[:, hs]
            acc_ref[:, hs] += jnp.exp(-c_old) * pv
            c_new = c_old + csum
        c_ref[:, hs] = c_new
        return jnp.min(c_new)

    def block(k_src, v_src, mask, tri_m, first):
        cmin = None
        for hh in range(SB_PAIR):
            hs = slice(hh * HEAD_W, (hh + 1) * HEAD_W)
            cm = head_block(hs, k_src[:, hs], v_src[:, hs], mask, tri_m, first)
            cmin = cm if cmin is None else jnp.minimum(cmin, cm)
        return cmin

    def diag_only(_):
        return block(k_ref.at[pl.ds(0, t_blk), :], v_ref.at[pl.ds(0, t_blk), :], col < row, tri, True)

    def diag_and_previous(_):
        k0 = pl.multiple_of((i - 1) * t_blk, t_blk)
        wrow = lax.broadcasted_iota(jnp.int32, (2 * t_blk, 2 * t_blk), 0)
        wcol = lax.broadcasted_iota(jnp.int32, (2 * t_blk, 2 * t_blk), 1)
        qrow = lax.broadcasted_iota(jnp.int32, (t_blk, 2 * t_blk), 0)
        kcol = lax.broadcasted_iota(jnp.int32, (t_blk, 2 * t_blk), 1)
        return block(k_ref.at[pl.ds(k0, 2 * t_blk), :], v_ref.at[pl.ds(k0, 2 * t_blk), :],
                     kcol < qrow + t_blk, (wrow > wcol).astype(_BF16), True)

    cmin0 = lax.cond(i == 0, diag_only, diag_and_previous, 0)

    def cond(carry):
        jj, cmin = carry
        return jnp.logical_and(jj < i, cmin < SB_DEAD)

    def body(carry):
        jj, _ = carry
        k0 = pl.multiple_of((i - 1 - jj) * t_blk, t_blk)
        cmin = block(k_ref.at[pl.ds(k0, t_blk), :], v_ref.at[pl.ds(k0, t_blk), :], None, tri, False)
        return jj + 1, cmin

    _, cmin = lax.while_loop(cond, body, (jnp.minimum(i, 1), cmin0))

    @pl.when(cmin < SB_DEAD)
    def _():
        mrow = lax.broadcasted_iota(jnp.int32, (META_PAD, META_PAD), 0)
        mcol = lax.broadcasted_iota(jnp.int32, (META_PAD, META_PAD), 1)
        valid = lax.broadcasted_iota(jnp.int32, (t_blk, META_PAD), 1) < N_META
        block(km_ref, vm_ref, valid, (mrow > mcol).astype(_BF16), False)

    o_ref[...] = acc_ref[...].astype(o_ref.dtype)


def _sb_attention(proj, meta_proj, b, s):
    t_blk = SB_T
    w = SB_PAIR * HEAD_W
    return pl.pallas_call(
        _sb_kernel,
        out_shape=jax.ShapeDtypeStruct((b, s, D_MODEL), _BF16),
        grid=(b, HEADS // SB_PAIR, s // t_blk),
        in_specs=[
            pl.BlockSpec((None, t_blk, w), lambda bi, h, i: (bi, i, SBQ // SB_PAIR + h)),
            pl.BlockSpec((None, s, w), lambda bi, h, i: (bi, 0, SBK // SB_PAIR + h)),
            pl.BlockSpec((None, s, w), lambda bi, h, i: (bi, 0, SBV // SB_PAIR + h)),
            pl.BlockSpec((META_PAD, w), lambda bi, h, i: (0, SBK // SB_PAIR + h)),
            pl.BlockSpec((META_PAD, w), lambda bi, h, i: (0, SBV // SB_PAIR + h)),
        ],
        out_specs=pl.BlockSpec((None, t_blk, w), lambda bi, h, i: (bi, i, h)),
        scratch_shapes=[
            pltpu.VMEM((t_blk, w), _F32),
            pltpu.VMEM((t_blk, w), _F32),
        ],
        compiler_params=pltpu.CompilerParams(
            dimension_semantics=("arbitrary", "arbitrary", "arbitrary"),
            vmem_limit_bytes=VMEM_LIMIT),
        name="sb_attn",
    )(proj, proj, proj, meta_proj, meta_proj)


def _df_kernel(slopes_ref, lam_ref, g_ref, q_ref, k_ref, v_ref, km_ref, vm_ref, o_ref,
               acc1, acc2, m1, m2, bias_ref, vext_ref, vmext_ref, kpos_ref, kn_ref):
    t_blk = DF_T
    h = pl.program_id(1)
    i = pl.program_id(2)
    slope = slopes_ref[h]

    @pl.when(i == 0)
    def _():
        vext_ref[:, :HEAD_W] = v_ref[...]
        vext_ref[:, HEAD_W:] = jnp.ones((vext_ref.shape[0], LANES), _BF16)
        vmext_ref[:, :HEAD_W] = vm_ref[...]
        vmext_ref[:, HEAD_W:] = jnp.ones((META_PAD, LANES), _BF16)
        row = lax.broadcasted_iota(jnp.int32, (t_blk, t_blk), 0)
        col = lax.broadcasted_iota(jnp.int32, (t_blk, t_blk), 1)
        visible = col < (row // CHUNK + 1) * CHUNK
        bias_ref[...] = jnp.where(visible, slope * (row - jnp.abs(row - col)).astype(_F32), -jnp.inf)
        c = lax.broadcasted_iota(jnp.int32, (2 * t_blk, LANES), 0)
        ln = lax.broadcasted_iota(jnp.int32, (2 * t_blk, LANES), 1)
        kpos = jnp.where(ln == 0, c // KPOS_RADIX, jnp.where(ln == 1, c % KPOS_RADIX, 0))
        kpos_ref[...] = kpos.astype(_F32).astype(_BF16)
        kn_ref[0] = jnp.maximum(_max_row_norm(k_ref[...]), _max_row_norm(km_ref[...]))

    q = q_ref[...]
    qs = q * DF_SCALE
    lane = lax.broadcasted_iota(jnp.int32, (t_blk, HEAD_W), 1)
    zero = jnp.zeros_like(q)
    q_lo = jnp.where(lane < DF_HALF, qs, zero)
    q_hi = jnp.where(lane >= DF_HALF, qs, zero)
    qx = jnp.where(lane == 0, slope * KPOS_RADIX, jnp.where(lane == 1, slope, 0.0)).astype(_BF16)
    q1 = jnp.concatenate([q_lo, qx], axis=1)
    q2 = jnp.concatenate([q_hi, qx], axis=1)
    score_bound = _max_row_norm(q) * kn_ref[0] * (DF_SCALE * DF_BOUND_SLACK)

    def tiled(stat, width):
        return jnp.concatenate([stat] * (width // LANES), axis=1)

    def update(y, vt, cb, acc, m, first):
        mx = jnp.broadcast_to(jnp.max(y, axis=1, keepdims=True), (t_blk, LANES)) + cb
        if first:
            m_new = mx
        else:
            m_old = m[...]
            m_new = jnp.maximum(m_old, mx)
        p = jnp.exp(y - tiled(m_new - cb, y.shape[1]))
        pv = jnp.dot(p.astype(_BF16), vt, preferred_element_type=_F32)
        if first:
            acc[...] = pv
        else:
            acc[...] = tiled(jnp.exp(m_old - m_new), pv.shape[1]) * acc[...] + pv
        m[...] = m_new
        return m_new

    def finish(m1_new, m2_new):
        t = lax.broadcasted_iota(jnp.int32, (t_blk, LANES), 0)
        return jnp.min(jnp.minimum(m1_new, m2_new) - slope * t.astype(_F32))

    def gap(dist, m_min):
        return score_bound - slope * dist.astype(_F32) - m_min

    def older_tile(kt, vt, width, n_valid, cb, dist, m_min):
        k_aug = jnp.concatenate([kt, kpos_ref[:width, :]], axis=1)

        def scores(qa):
            y = lax.dot_general(qa, k_aug, _NT, preferred_element_type=_F32)
            if n_valid is not None:
                valid = lax.broadcasted_iota(jnp.int32, (t_blk, width), 1) < n_valid
                y = jnp.where(valid, y, -jnp.inf)
            return y

        def fixed_reference(_):
            for qa, acc, m in ((q1, acc1, m1), (q2, acc2, m2)):
                y = scores(qa)
                p = jnp.exp(y - tiled(m[...] - cb, width))
                acc[...] += jnp.dot(p.astype(_BF16), vt, preferred_element_type=_F32)
            return m_min

        def moving_reference(_):
            y1 = scores(q1)
            y2 = scores(q2)
            return finish(update(y1, vt, cb, acc1, m1, False), update(y2, vt, cb, acc2, m2, False))

        return lax.cond(gap(dist, m_min) < DF_SAFE, fixed_reference, moving_reference, 0)

    def alive(dist, m_min):
        return gap(dist, m_min) > -DF_DEAD

    d0 = pl.multiple_of(i * t_blk, t_blk)
    kd = k_ref[pl.ds(d0, t_blk), :]
    vd = vext_ref[pl.ds(d0, t_blk), :]
    yd1 = lax.dot_general(q_lo, kd, _NT, preferred_element_type=_F32) + bias_ref[...]
    yd2 = lax.dot_general(q_hi, kd, _NT, preferred_element_type=_F32) + bias_ref[...]
    mmin0 = finish(update(yd1, vd, 0.0, acc1, m1, True), update(yd2, vd, 0.0, acc2, m2, True))

    odd = i % 2

    def single(_):
        k0 = pl.multiple_of((i - 1) * t_blk, t_blk)
        return older_tile(k_ref[pl.ds(k0, t_blk), :], vext_ref[pl.ds(k0, t_blk), :],
                          t_blk, None, -slope * t_blk, jnp.int32(1), mmin0)

    mmin1 = lax.cond(jnp.logical_and(odd == 1, alive(jnp.int32(1), mmin0)),
                     single, lambda _: mmin0, 0)

    def cond(carry):
        jj, m_min = carry
        return jnp.logical_and(jj < i, alive(jj * t_blk + 1, m_min))

    def body(carry):
        jj, m_min = carry
        k0 = pl.multiple_of((i - 2 - jj) * t_blk, t_blk)
        cb = -slope * ((jj + 2) * t_blk).astype(_F32)
        m_min = older_tile(k_ref[pl.ds(k0, 2 * t_blk), :], vext_ref[pl.ds(k0, 2 * t_blk), :],
                           2 * t_blk, None, cb, jj * t_blk + 1, m_min)
        return jj + 2, m_min

    _, mmin = lax.while_loop(cond, body, (odd, mmin1))

    @pl.when(alive(i * t_blk + 1, mmin))
    def _():
        cb_meta = -slope * (i * t_blk + N_META).astype(_F32)
        older_tile(km_ref[...], vmext_ref[...], META_PAD, N_META, cb_meta, i * t_blk + 1, mmin)

    lv = lam_ref[...]
    d1 = jnp.sum(lv[0:1, :] * lv[1:2, :], axis=1, keepdims=True)
    d2 = jnp.sum(lv[2:3, :] * lv[3:4, :], axis=1, keepdims=True)
    lam = jnp.exp(d1) - jnp.exp(d2) + LAM_INIT
    a1 = acc1[...]
    a2 = acc2[...]
    o = a1[:, :HEAD_W] * (1.0 / a1[:, HEAD_W:]) - lam * (a2[:, :HEAD_W] * (1.0 / a2[:, HEAD_W:]))
    o_ref[...] = (_rms(o, g_ref[...]) * (1.0 - LAM_INIT)).astype(o_ref.dtype)


def _df_attention(proj, meta_proj, slopes, lamvec, subln_g, b, s):
    t_blk = DF_T
    return pl.pallas_call(
        _df_kernel,
        out_shape=jax.ShapeDtypeStruct((b, s, D_MODEL), _BF16),
        grid=(b, HEADS, s // t_blk),
        in_specs=[
            pl.BlockSpec(memory_space=pltpu.SMEM),
            pl.BlockSpec((4, DF_HALF), lambda bi, h, i: (0, 0)),
            pl.BlockSpec((1, HEAD_W), lambda bi, h, i: (0, 0)),
            pl.BlockSpec((None, t_blk, HEAD_W), lambda bi, h, i: (bi, i, DFQ + h)),
            pl.BlockSpec((None, s, HEAD_W), lambda bi, h, i: (bi, 0, DFK + h)),
            pl.BlockSpec((None, s, HEAD_W), lambda bi, h, i: (bi, 0, DFV + h)),
            pl.BlockSpec((META_PAD, HEAD_W), lambda bi, h, i: (0, DFK + h)),
            pl.BlockSpec((META_PAD, HEAD_W), lambda bi, h, i: (0, DFV + h)),
        ],
        out_specs=pl.BlockSpec((None, t_blk, HEAD_W), lambda bi, h, i: (bi, i, h)),
        scratch_shapes=[
            pltpu.VMEM((t_blk, HEAD_W + LANES), _F32),
            pltpu.VMEM((t_blk, HEAD_W + LANES), _F32),
            pltpu.VMEM((t_blk, LANES), _F32),
            pltpu.VMEM((t_blk, LANES), _F32),
            pltpu.VMEM((t_blk, t_blk), _F32),
            pltpu.VMEM((s, HEAD_W + LANES), _BF16),
            pltpu.VMEM((META_PAD, HEAD_W + LANES), _BF16),
            pltpu.VMEM((2 * t_blk, LANES), _BF16),
            pltpu.SMEM((1,), _F32),
        ],
        compiler_params=pltpu.CompilerParams(
            dimension_semantics=("arbitrary", "arbitrary", "arbitrary"),
            vmem_limit_bytes=VMEM_LIMIT),
        name="df_attn",
    )(slopes, lamvec, subln_g, proj, proj, proj, meta_proj, meta_proj)


def _merge_kernel(x_ref, g_ref, osb_ref, odf_ref, wg_ref, bg_ref, wsb_ref, wdf_ref, wo_ref,
                  h_ref, merged_ref, *, n_chunk):
    x = x_ref[...]
    xn = _rms(x, g_ref[...]).astype(_BF16)
    osb = osb_ref[...]
    odf = odf_ref[...]
    for n in range(D_MODEL // n_chunk):
        sa = slice(n * n_chunk, (n + 1) * n_chunk)
        sb = slice(D_MODEL + n * n_chunk, D_MODEL + (n + 1) * n_chunk)
        ga = jax.nn.sigmoid(jnp.dot(xn, wg_ref[:, sa], preferred_element_type=_F32) + bg_ref[:, sa])
        gb = jax.nn.sigmoid(jnp.dot(xn, wg_ref[:, sb], preferred_element_type=_F32) + bg_ref[:, sb])
        ysb = jnp.dot(osb, wsb_ref[:, sa], preferred_element_type=_F32)
        ydf = jnp.dot(odf, wdf_ref[:, sa], preferred_element_type=_F32)
        merged_ref[:, sa] = (ga * ysb + gb * ydf).astype(_BF16)
    h_ref[...] = x + jnp.dot(merged_ref[...], wo_ref[...], preferred_element_type=_F32)


def _merge(x2, g, osb, odf, wg, bg, wsb, wdf, wo):
    m = x2.shape[0]
    const = lambda shape: pl.BlockSpec(shape, lambda i: (0, 0), pipeline_mode=pl.Buffered(1))
    rows = lambda: pl.BlockSpec((TM, D_MODEL), lambda i: (i, 0))
    return pl.pallas_call(
        functools.partial(_merge_kernel, n_chunk=512),
        out_shape=jax.ShapeDtypeStruct((m, D_MODEL), _F32),
        grid=(m // TM,),
        in_specs=[
            rows(), const((1, D_MODEL)), rows(), rows(),
            const((D_MODEL, 2 * D_MODEL)), const((1, 2 * D_MODEL)),
            const((D_MODEL, D_MODEL)), const((D_MODEL, D_MODEL)), const((D_MODEL, D_MODEL)),
        ],
        out_specs=rows(),
        scratch_shapes=[pltpu.VMEM((TM, D_MODEL), _BF16)],
        compiler_params=pltpu.CompilerParams(
            dimension_semantics=("arbitrary",), vmem_limit_bytes=VMEM_LIMIT),
        name="merge",
    )(x2, g, osb, odf, wg, bg, wsb, wdf, wo)


def _ffn_kernel(h_ref, gf_ref, wg_ref, wu_ref, wd_ref, gl_ref, y_ref, acc_ref):
    h = h_ref[...]
    hn = _rms(h, gf_ref[...]).astype(_BF16)
    for n in range(D_FF // FF_CHUNK):
        sl = slice(n * FF_CHUNK, (n + 1) * FF_CHUNK)
        gate = jnp.dot(hn, wg_ref[:, sl], preferred_element_type=_F32)
        up = jnp.dot(hn, wu_ref[:, sl], preferred_element_type=_F32)
        act = (jax.nn.silu(gate) * up).astype(_BF16)
        down = jnp.dot(act, wd_ref[sl, :], preferred_element_type=_F32)
        if n == 0:
            acc_ref[...] = down
        else:
            acc_ref[...] += down
    y_ref[...] = _rms(h + acc_ref[...], gl_ref[...])


def _ffn(h1, gf, wg, wu, wd, gl):
    m = h1.shape[0]
    const = lambda shape: pl.BlockSpec(shape, lambda i: (0, 0), pipeline_mode=pl.Buffered(1))
    rows = lambda: pl.BlockSpec((TM, D_MODEL), lambda i: (i, 0))
    return pl.pallas_call(
        _ffn_kernel,
        out_shape=jax.ShapeDtypeStruct((m, D_MODEL), _F32),
        grid=(m // TM,),
        in_specs=[
            rows(), const((1, D_MODEL)),
            const((D_MODEL, D_FF)), const((D_MODEL, D_FF)), const((D_FF, D_MODEL)),
            const((1, D_MODEL)),
        ],
        out_specs=rows(),
        scratch_shapes=[pltpu.VMEM((TM, D_MODEL), _F32)],
        compiler_params=pltpu.CompilerParams(
            dimension_semantics=("arbitrary",), vmem_limit_bytes=VMEM_LIMIT),
        name="ffn",
    )(h1, gf, wg, wu, wd, gl)


def kernel(x, meta, norm_mix_g, w_in, w_gate, b_gate, lam_q1, lam_k1, lam_q2, lam_k2, subln_g,
           w_br_sb, w_br_df, w_out, norm_ffn_g, w_ffn_gate, w_ffn_up, w_ffn_down, norm_final_g):
    b, s, d = x.shape
    assert d == D_MODEL and s % SB_T == 0 and s % DF_T == 0 and (b * s) % TM == 0
    assert meta.shape == (N_META, D_MODEL) and w_in.shape[0] == 1

    bf = lambda w: w.astype(_BF16)
    x2 = x.reshape(b * s, d)
    g_mix = norm_mix_g[0][None, :]
    w_in_b = bf(w_in[0])

    proj = _in_proj(x2, g_mix, w_in_b, TM).reshape(b, s, IN_W)
    meta_proj = _in_proj(meta.astype(x.dtype), g_mix, w_in_b, N_META)
    meta_proj = jnp.pad(meta_proj, ((0, META_PAD - N_META), (0, 0)))

    slopes = jnp.exp2(-8.0 * (jnp.arange(HEADS) + 1) / HEADS).astype(_F32)
    lamvec = jnp.stack([lam_q1[0], lam_k1[0], lam_q2[0], lam_k2[0]]).astype(_F32)

    o_sb = _sb_attention(proj, meta_proj, b, s).reshape(b * s, d)
    o_df = _df_attention(proj, meta_proj, slopes, lamvec, subln_g[0][None, :], b, s).reshape(b * s, d)

    h1 = _merge(x2, g_mix, o_sb, o_df, bf(w_gate[0]), b_gate[0][None, :],
                bf(w_br_sb[0]), bf(w_br_df[0]), bf(w_out[0]))
    y = _ffn(h1, norm_ffn_g[0][None, :], bf(w_ffn_gate[0]), bf(w_ffn_up[0]), bf(w_ffn_down[0]),
             norm_final_g[None, :])
    return y.reshape(b, s, d)
```

```python
import functools
import math

import jax
import jax.numpy as jnp
from jax import lax
from jax.experimental import pallas as pl
from jax.experimental.pallas import tpu as pltpu

D_MODEL = 1024
N_META = 16
CHUNK = 64
HEADS = 8
HEAD_W = 128
DF_HALF = 64
IN_W = 6 * D_MODEL
D_FF = 2816
EPS = 1e-6
LAM_INIT = 0.8 - 0.6 * math.exp(-0.3 * 0)
LOG2E = 1.4426950408889634
SB_SCALE = HEAD_W ** -0.5
DF_SCALE = DF_HALF ** -0.5

SBQ, SBK, SBV, DFQ, DFK, DFV = 0, 8, 16, 24, 32, 40

LANES = 128
META_PAD = LANES
SB_T = 256
SB_PAIR = 4
DF_T = 512
TM = 512
FF_CHUNK = 256
VMEM_LIMIT = 48 * 1024 * 1024

SB_DEAD = 110.0
DF_DEAD = 112.0
DF_BOUND_SLACK = 1.01
KPOS_RADIX = 32
DF_SAFE = 60.0

_BF16 = jnp.bfloat16
_F32 = jnp.float32
_NT = (((1,), (1,)), ((), ()))


def _rms(xf, g):
    ms = jnp.mean(xf * xf, axis=-1, keepdims=True)
    return xf * lax.rsqrt(ms + EPS) * g


def _max_row_norm(x):
    xf = x.astype(_F32)
    return jnp.max(jnp.sqrt(jnp.sum(xf * xf, axis=1, keepdims=True)))


def _inproj_kernel(x_ref, g_ref, w_ref, o_ref, *, n_chunk):
    xn = _rms(x_ref[...], g_ref[...]).astype(_BF16)
    for n in range(IN_W // n_chunk):
        sl = slice(n * n_chunk, (n + 1) * n_chunk)
        o_ref[:, sl] = jnp.dot(xn, w_ref[:, sl], preferred_element_type=_F32).astype(_BF16)


def _in_proj(x2, g, w, tm):
    m = x2.shape[0]
    return pl.pallas_call(
        functools.partial(_inproj_kernel, n_chunk=512),
        out_shape=jax.ShapeDtypeStruct((m, IN_W), _BF16),
        grid=(m // tm,),
        in_specs=[
            pl.BlockSpec((tm, D_MODEL), lambda i: (i, 0)),
            pl.BlockSpec((1, D_MODEL), lambda i: (0, 0)),
            pl.BlockSpec((D_MODEL, IN_W), lambda i: (0, 0), pipeline_mode=pl.Buffered(1)),
        ],
        out_specs=pl.BlockSpec((tm, IN_W), lambda i: (i, 0)),
        compiler_params=pltpu.CompilerParams(
            dimension_semantics=("arbitrary",), vmem_limit_bytes=VMEM_LIMIT),
        name="in_proj",
    )(x2, g, w)


def _sb_kernel(q_ref, k_ref, v_ref, km_ref, vm_ref, o_ref, acc_ref, c_ref):
    t_blk = SB_T
    i = pl.program_id(2)

    row = lax.broadcasted_iota(jnp.int32, (t_blk, t_blk), 0)
    col = lax.broadcasted_iota(jnp.int32, (t_blk, t_blk), 1)
    tri = (row > col).astype(_BF16)

    def block(k_src, v_src, mask, tri_m, first):
        heads = [slice(hh * HEAD_W, (hh + 1) * HEAD_W) for hh in range(SB_PAIR)]

        def scores(hs):
            return lax.dot_general(q_ref[:, hs], k_src[:, hs], _NT, preferred_element_type=_F32) * SB_SCALE

        def logits(z):
            t = jnp.log(1.0 + jnp.exp2(jnp.abs(z) * (-LOG2E)))
            sp = jnp.maximum(z, 0.0) + t
            lsig = jnp.minimum(z, 0.0) - t
            if mask is not None:
                sp = jnp.where(mask, sp, 0.0)
            return sp, lsig

        def weights(sp, lsig):
            suffix = jnp.dot(sp.astype(_BF16), tri_m, preferred_element_type=_F32)
            a = jnp.exp(lsig - suffix)
            if mask is not None:
                a = jnp.where(mask, a, 0.0)
            return a.astype(_BF16)

        def accumulate(hs, a, sp):
            pv = jnp.dot(a, v_src[:, hs], preferred_element_type=_F32)
            csum = jnp.broadcast_to(jnp.sum(sp, axis=1, keepdims=True), (t_blk, LANES))
            if first:
                acc_ref[:, hs] = pv
                c_new = csum
            else:
                c_old = c_ref[:, hs]
                acc_ref[:, hs] += jnp.exp(-c_old) * pv
                c_new = c_old + csum
            c_ref[:, hs] = c_new
            return jnp.min(c_new)

        n = SB_PAIR
        z, sl, a, cmin = [None] * n, [None] * n, [None] * n, None
        for step in range(n + 3):
            if step - 3 >= 0:
                cm = accumulate(heads[step - 3], a[step - 3], sl[step - 3][0])
                cmin = cm if cmin is None else jnp.minimum(cmin, cm)
            if 0 <= step - 2 < n:
                a[step - 2] = weights(*sl[step - 2])
            if 0 <= step - 1 < n:
                sl[step - 1] = logits(z[step - 1])
            if step < n:
                z[step] = scores(heads[step])
        return cmin

    def diag_only(_):
        return block(k_ref.at[pl.ds(0, t_blk), :], v_ref.at[pl.ds(0, t_blk), :], col < row, tri, True)

    def diag_and_previous(_):
        k0 = pl.multiple_of((i - 1) * t_blk, t_blk)
        wrow = lax.broadcasted_iota(jnp.int32, (2 * t_blk, 2 * t_blk), 0)
        wcol = lax.broadcasted_iota(jnp.int32, (2 * t_blk, 2 * t_blk), 1)
        qrow = lax.broadcasted_iota(jnp.int32, (t_blk, 2 * t_blk), 0)
        kcol = lax.broadcasted_iota(jnp.int32, (t_blk, 2 * t_blk), 1)
        return block(k_ref.at[pl.ds(k0, 2 * t_blk), :], v_ref.at[pl.ds(k0, 2 * t_blk), :],
                     kcol < qrow + t_blk, (wrow > wcol).astype(_BF16), True)

    cmin0 = lax.cond(i == 0, diag_only, diag_and_previous, 0)

    def cond(carry):
        jj, cmin = carry
        return jnp.logical_and(jj < i, cmin < SB_DEAD)

    def body(carry):
        jj, _ = carry
        k0 = pl.multiple_of((i - 1 - jj) * t_blk, t_blk)
        cmin = block(k_ref.at[pl.ds(k0, t_blk), :], v_ref.at[pl.ds(k0, t_blk), :], None, tri, False)
        return jj + 1, cmin

    _, cmin = lax.while_loop(cond, body, (jnp.minimum(i, 1), cmin0))

    @pl.when(cmin < SB_DEAD)
    def _():
        mrow = lax.broadcasted_iota(jnp.int32, (META_PAD, META_PAD), 0)
        mcol = lax.broadcasted_iota(jnp.int32, (META_PAD, META_PAD), 1)
        valid = lax.broadcasted_iota(jnp.int32, (t_blk, META_PAD), 1) < N_META
        block(km_ref, vm_ref, valid, (mrow > mcol).astype(_BF16), False)

    o_ref[...] = acc_ref[...].astype(o_ref.dtype)


def _sb_attention(proj, meta_proj, b, s):
    t_blk = SB_T
    w = SB_PAIR * HEAD_W
    return pl.pallas_call(
        _sb_kernel,
        out_shape=jax.ShapeDtypeStruct((b, s, D_MODEL), _BF16),
        grid=(b, HEADS // SB_PAIR, s // t_blk),
        in_specs=[
            pl.BlockSpec((None, t_blk, w), lambda bi, h, i: (bi, i, SBQ // SB_PAIR + h)),
            pl.BlockSpec((None, s, w), lambda bi, h, i: (bi, 0, SBK // SB_PAIR + h)),
            pl.BlockSpec((None, s, w), lambda bi, h, i: (bi, 0, SBV // SB_PAIR + h)),
            pl.BlockSpec((META_PAD, w), lambda bi, h, i: (0, SBK // SB_PAIR + h)),
            pl.BlockSpec((META_PAD, w), lambda bi, h, i: (0, SBV // SB_PAIR + h)),
        ],
        out_specs=pl.BlockSpec((None, t_blk, w), lambda bi, h, i: (bi, i, h)),
        scratch_shapes=[
            pltpu.VMEM((t_blk, w), _F32),
            pltpu.VMEM((t_blk, w), _F32),
        ],
        compiler_params=pltpu.CompilerParams(
            dimension_semantics=("arbitrary", "arbitrary", "arbitrary"),
            vmem_limit_bytes=VMEM_LIMIT),
        name="sb_attn",
    )(proj, proj, proj, meta_proj, meta_proj)


def _df_kernel(slopes_ref, lam_ref, g_ref, q_ref, k_ref, v_ref, km_ref, vm_ref, o_ref,
               acc1, acc2, m1, m2, bias_ref, vext_ref, vmext_ref, kpos_ref, kn_ref):
    t_blk = DF_T
    h = pl.program_id(1)
    i = pl.program_id(2)
    slope = slopes_ref[h]

    @pl.when(i == 0)
    def _():
        vext_ref[:, :HEAD_W] = v_ref[...]
        vext_ref[:, HEAD_W:] = jnp.ones((vext_ref.shape[0], LANES), _BF16)
        vmext_ref[:, :HEAD_W] = vm_ref[...]
        vmext_ref[:, HEAD_W:] = jnp.ones((META_PAD, LANES), _BF16)
        row = lax.broadcasted_iota(jnp.int32, (t_blk, t_blk), 0)
        col = lax.broadcasted_iota(jnp.int32, (t_blk, t_blk), 1)
        visible = col < (row // CHUNK + 1) * CHUNK
        bias_ref[...] = jnp.where(visible, slope * (row - jnp.abs(row - col)).astype(_F32), -jnp.inf)
        c = lax.broadcasted_iota(jnp.int32, (2 * t_blk, LANES), 0)
        ln = lax.broadcasted_iota(jnp.int32, (2 * t_blk, LANES), 1)
        kpos = jnp.where(ln == 0, c // KPOS_RADIX, jnp.where(ln == 1, c % KPOS_RADIX, 0))
        kpos_ref[...] = kpos.astype(_F32).astype(_BF16)
        kn_ref[0] = jnp.maximum(_max_row_norm(k_ref[...]), _max_row_norm(km_ref[...]))

    q = q_ref[...]
    qs = q * DF_SCALE
    lane = lax.broadcasted_iota(jnp.int32, (t_blk, HEAD_W), 1)
    zero = jnp.zeros_like(q)
    q_lo = jnp.where(lane < DF_HALF, qs, zero)
    q_hi = jnp.where(lane >= DF_HALF, qs, zero)
    qx = jnp.where(lane == 0, slope * KPOS_RADIX, jnp.where(lane == 1, slope, 0.0)).astype(_BF16)
    q1 = jnp.concatenate([q_lo, qx], axis=1)
    q2 = jnp.concatenate([q_hi, qx], axis=1)
    score_bound = _max_row_norm(q) * kn_ref[0] * (DF_SCALE * DF_BOUND_SLACK)

    def tiled(stat, width):
        return jnp.concatenate([stat] * (width // LANES), axis=1)

    def update(y, vt, cb, acc, m, first):
        mx = jnp.broadcast_to(jnp.max(y, axis=1, keepdims=True), (t_blk, LANES)) + cb
        if first:
            m_new = mx
        else:
            m_old = m[...]
            m_new = jnp.maximum(m_old, mx)
        p = jnp.exp(y - tiled(m_new - cb, y.shape[1]))
        pv = jnp.dot(p.astype(_BF16), vt, preferred_element_type=_F32)
        if first:
            acc[...] = pv
        else:
            acc[...] = tiled(jnp.exp(m_old - m_new), pv.shape[1]) * acc[...] + pv
        m[...] = m_new
        return m_new

    def finish(m1_new, m2_new):
        t = lax.broadcasted_iota(jnp.int32, (t_blk, LANES), 0)
        return jnp.min(jnp.minimum(m1_new, m2_new) - slope * t.astype(_F32))

    def gap(dist, m_min):
        return score_bound - slope * dist.astype(_F32) - m_min

    def older_tile(kt, vt, width, n_valid, cb, dist, m_min):
        k_aug = jnp.concatenate([kt, kpos_ref[:width, :]], axis=1)

        def scores(qa):
            y = lax.dot_general(qa, k_aug, _NT, preferred_element_type=_F32)
            if n_valid is not None:
                valid = lax.broadcasted_iota(jnp.int32, (t_blk, width), 1) < n_valid
                y = jnp.where(valid, y, -jnp.inf)
            return y

        def fixed_reference(_):
            for qa, acc, m in ((q1, acc1, m1), (q2, acc2, m2)):
                y = scores(qa)
                p = jnp.exp(y - tiled(m[...] - cb, width))
                acc[...] += jnp.dot(p.astype(_BF16), vt, preferred_element_type=_F32)
            return m_min

        def moving_reference(_):
            y1 = scores(q1)
            y2 = scores(q2)
            return finish(update(y1, vt, cb, acc1, m1, False), update(y2, vt, cb, acc2, m2, False))

        return lax.cond(gap(dist, m_min) < DF_SAFE, fixed_reference, moving_reference, 0)

    def alive(dist, m_min):
        return gap(dist, m_min) > -DF_DEAD

    d0 = pl.multiple_of(i * t_blk, t_blk)
    kd = k_ref[pl.ds(d0, t_blk), :]
    vd = vext_ref[pl.ds(d0, t_blk), :]
    yd1 = lax.dot_general(q_lo, kd, _NT, preferred_element_type=_F32) + bias_ref[...]
    yd2 = lax.dot_general(q_hi, kd, _NT, preferred_element_type=_F32) + bias_ref[...]
    mmin0 = finish(update(yd1, vd, 0.0, acc1, m1, True), update(yd2, vd, 0.0, acc2, m2, True))

    odd = i % 2

    def single(_):
        k0 = pl.multiple_of((i - 1) * t_blk, t_blk)
        return older_tile(k_ref[pl.ds(k0, t_blk), :], vext_ref[pl.ds(k0, t_blk), :],
                          t_blk, None, -slope * t_blk, jnp.int32(1), mmin0)

    mmin1 = lax.cond(jnp.logical_and(odd == 1, alive(jnp.int32(1), mmin0)),
                     single, lambda _: mmin0, 0)

    def cond(carry):
        jj, m_min = carry
        return jnp.logical_and(jj < i, alive(jj * t_blk + 1, m_min))

    def body(carry):
        jj, m_min = carry
        k0 = pl.multiple_of((i - 2 - jj) * t_blk, t_blk)
        cb = -slope * ((jj + 2) * t_blk).astype(_F32)
        m_min = older_tile(k_ref[pl.ds(k0, 2 * t_blk), :], vext_ref[pl.ds(k0, 2 * t_blk), :],
                           2 * t_blk, None, cb, jj * t_blk + 1, m_min)
        return jj + 2, m_min

    _, mmin = lax.while_loop(cond, body, (odd, mmin1))

    @pl.when(alive(i * t_blk + 1, mmin))
    def _():
        cb_meta = -slope * (i * t_blk + N_META).astype(_F32)
        older_tile(km_ref[...], vmext_ref[...], META_PAD, N_META, cb_meta, i * t_blk + 1, mmin)

    lv = lam_ref[...]
    d1 = jnp.sum(lv[0:1, :] * lv[1:2, :], axis=1, keepdims=True)
    d2 = jnp.sum(lv[2:3, :] * lv[3:4, :], axis=1, keepdims=True)
    lam = jnp.exp(d1) - jnp.exp(d2) + LAM_INIT
    a1 = acc1[...]
    a2 = acc2[...]
    o = a1[:, :HEAD_W] * (1.0 / a1[:, HEAD_W:]) - lam * (a2[:, :HEAD_W] * (1.0 / a2[:, HEAD_W:]))
    o_ref[...] = (_rms(o, g_ref[...]) * (1.0 - LAM_INIT)).astype(o_ref.dtype)


def _df_attention(proj, meta_proj, slopes, lamvec, subln_g, b, s):
    t_blk = DF_T
    return pl.pallas_call(
        _df_kernel,
        out_shape=jax.ShapeDtypeStruct((b, s, D_MODEL), _BF16),
        grid=(b, HEADS, s // t_blk),
        in_specs=[
            pl.BlockSpec(memory_space=pltpu.SMEM),
            pl.BlockSpec((4, DF_HALF), lambda bi, h, i: (0, 0)),
            pl.BlockSpec((1, HEAD_W), lambda bi, h, i: (0, 0)),
            pl.BlockSpec((None, t_blk, HEAD_W), lambda bi, h, i: (bi, i, DFQ + h)),
            pl.BlockSpec((None, s, HEAD_W), lambda bi, h, i: (bi, 0, DFK + h)),
            pl.BlockSpec((None, s, HEAD_W), lambda bi, h, i: (bi, 0, DFV + h)),
            pl.BlockSpec((META_PAD, HEAD_W), lambda bi, h, i: (0, DFK + h)),
            pl.BlockSpec((META_PAD, HEAD_W), lambda bi, h, i: (0, DFV + h)),
        ],
        out_specs=pl.BlockSpec((None, t_blk, HEAD_W), lambda bi, h, i: (bi, i, h)),
        scratch_shapes=[
            pltpu.VMEM((t_blk, HEAD_W + LANES), _F32),
            pltpu.VMEM((t_blk, HEAD_W + LANES), _F32),
            pltpu.VMEM((t_blk, LANES), _F32),
            pltpu.VMEM((t_blk, LANES), _F32),
            pltpu.VMEM((t_blk, t_blk), _F32),
            pltpu.VMEM((s, HEAD_W + LANES), _BF16),
            pltpu.VMEM((META_PAD, HEAD_W + LANES), _BF16),
            pltpu.VMEM((2 * t_blk, LANES), _BF16),
            pltpu.SMEM((1,), _F32),
        ],
        compiler_params=pltpu.CompilerParams(
            dimension_semantics=("arbitrary", "arbitrary", "arbitrary"),
            vmem_limit_bytes=VMEM_LIMIT),
        name="df_attn",
    )(slopes, lamvec, subln_g, proj, proj, proj, meta_proj, meta_proj)


def _merge_kernel(x_ref, g_ref, osb_ref, odf_ref, wg_ref, bg_ref, wsb_ref, wdf_ref, wo_ref,
                  h_ref, merged_ref, *, n_chunk):
    x = x_ref[...]
    xn = _rms(x, g_ref[...]).astype(_BF16)
    osb = osb_ref[...]
    odf = odf_ref[...]
    for n in range(D_MODEL // n_chunk):
        sa = slice(n * n_chunk, (n + 1) * n_chunk)
        sb = slice(D_MODEL + n * n_chunk, D_MODEL + (n + 1) * n_chunk)
        ga = jax.nn.sigmoid(jnp.dot(xn, wg_ref[:, sa], preferred_element_type=_F32) + bg_ref[:, sa])
        gb = jax.nn.sigmoid(jnp.dot(xn, wg_ref[:, sb], preferred_element_type=_F32) + bg_ref[:, sb])
        ysb = jnp.dot(osb, wsb_ref[:, sa], preferred_element_type=_F32)
        ydf = jnp.dot(odf, wdf_ref[:, sa], preferred_element_type=_F32)
        merged_ref[:, sa] = (ga * ysb + gb * ydf).astype(_BF16)
    h_ref[...] = x + jnp.dot(merged_ref[...], wo_ref[...], preferred_element_type=_F32)


def _merge(x2, g, osb, odf, wg, bg, wsb, wdf, wo):
    m = x2.shape[0]
    const = lambda shape: pl.BlockSpec(shape, lambda i: (0, 0), pipeline_mode=pl.Buffered(1))
    rows = lambda: pl.BlockSpec((TM, D_MODEL), lambda i: (i, 0))
    return pl.pallas_call(
        functools.partial(_merge_kernel, n_chunk=512),
        out_shape=jax.ShapeDtypeStruct((m, D_MODEL), _F32),
        grid=(m // TM,),
        in_specs=[
            rows(), const((1, D_MODEL)), rows(), rows(),
            const((D_MODEL, 2 * D_MODEL)), const((1, 2 * D_MODEL)),
            const((D_MODEL, D_MODEL)), const((D_MODEL, D_MODEL)), const((D_MODEL, D_MODEL)),
        ],
        out_specs=rows(),
        scratch_shapes=[pltpu.VMEM((TM, D_MODEL), _BF16)],
        compiler_params=pltpu.CompilerParams(
            dimension_semantics=("arbitrary",), vmem_limit_bytes=VMEM_LIMIT),
        name="merge",
    )(x2, g, osb, odf, wg, bg, wsb, wdf, wo)


def _ffn_kernel(h_ref, gf_ref, wg_ref, wu_ref, wd_ref, gl_ref, y_ref, acc_ref):
    h = h_ref[...]
    hn = _rms(h, gf_ref[...]).astype(_BF16)
    for n in range(D_FF // FF_CHUNK):
        sl = slice(n * FF_CHUNK, (n + 1) * FF_CHUNK)
        gate = jnp.dot(hn, wg_ref[:, sl], preferred_element_type=_F32)
        up = jnp.dot(hn, wu_ref[:, sl], preferred_element_type=_F32)
        act = (jax.nn.silu(gate) * up).astype(_BF16)
        down = jnp.dot(act, wd_ref[sl, :], preferred_element_type=_F32)
        if n == 0:
            acc_ref[...] = down
        else:
            acc_ref[...] += down
    y_ref[...] = _rms(h + acc_ref[...], gl_ref[...])


def _ffn(h1, gf, wg, wu, wd, gl):
    m = h1.shape[0]
    const = lambda shape: pl.BlockSpec(shape, lambda i: (0, 0), pipeline_mode=pl.Buffered(1))
    rows = lambda: pl.BlockSpec((TM, D_MODEL), lambda i: (i, 0))
    return pl.pallas_call(
        _ffn_kernel,
        out_shape=jax.ShapeDtypeStruct((m, D_MODEL), _F32),
        grid=(m // TM,),
        in_specs=[
            rows(), const((1, D_MODEL)),
            const((D_MODEL, D_FF)), const((D_MODEL, D_FF)), const((D_FF, D_MODEL)),
            const((1, D_MODEL)),
        ],
        out_specs=rows(),
        scratch_shapes=[pltpu.VMEM((TM, D_MODEL), _F32)],
        compiler_params=pltpu.CompilerParams(
            dimension_semantics=("arbitrary",), vmem_limit_bytes=VMEM_LIMIT),
        name="ffn",
    )(h1, gf, wg, wu, wd, gl)


def kernel(x, meta, norm_mix_g, w_in, w_gate, b_gate, lam_q1, lam_k1, lam_q2, lam_k2, subln_g,
           w_br_sb, w_br_df, w_out, norm_ffn_g, w_ffn_gate, w_ffn_up, w_ffn_down, norm_final_g):
    b, s, d = x.shape
    assert d == D_MODEL and s % SB_T == 0 and s % DF_T == 0 and (b * s) % TM == 0
    assert meta.shape == (N_META, D_MODEL) and w_in.shape[0] == 1

    bf = lambda w: w.astype(_BF16)
    x2 = x.reshape(b * s, d)
    g_mix = norm_mix_g[0][None, :]
    w_in_b = bf(w_in[0])

    proj = _in_proj(x2, g_mix, w_in_b, TM).reshape(b, s, IN_W)
    meta_proj = _in_proj(meta.astype(x.dtype), g_mix, w_in_b, N_META)
    meta_proj = jnp.pad(meta_proj, ((0, META_PAD - N_META), (0, 0)))

    slopes = jnp.exp2(-8.0 * (jnp.arange(HEADS) + 1) / HEADS).astype(_F32)
    lamvec = jnp.stack([lam_q1[0], lam_k1[0], lam_q2[0], lam_k2[0]]).astype(_F32)

    o_sb = _sb_attention(proj, meta_proj, b, s).reshape(b * s, d)
    o_df = _df_attention(proj, meta_proj, slopes, lamvec, subln_g[0][None, :], b, s).reshape(b * s, d)

    h1 = _merge(x2, g_mix, o_sb, o_df, bf(w_gate[0]), b_gate[0][None, :],
                bf(w_br_sb[0]), bf(w_br_df[0]), bf(w_out[0]))
    y = _ffn(h1, norm_ffn_g[0][None, :], bf(w_ffn_gate[0]), bf(w_ffn_up[0]), bf(w_ffn_down[0]),
             norm_final_g[None, :])
    return y.reshape(b, s, d)
```

```python
import functools
import math

import jax
import jax.numpy as jnp
from jax import lax
from jax.experimental import pallas as pl
from jax.experimental.pallas import tpu as pltpu

D_MODEL = 1024
N_META = 16
CHUNK = 64
HEADS = 8
HEAD_W = 128
DF_HALF = 64
IN_W = 6 * D_MODEL
D_FF = 2816
EPS = 1e-6
LAM_INIT = 0.8 - 0.6 * math.exp(-0.3 * 0)
LOG2E = 1.4426950408889634
SB_SCALE = HEAD_W ** -0.5
DF_SCALE = DF_HALF ** -0.5

SBQ, SBK, SBV, DFQ, DFK, DFV = 0, 8, 16, 24, 32, 40

LANES = 128
META_PAD = LANES
SB_T = 256
SB_PAIR = 4
DF_T = 512
TM = 512
FF_CHUNK = 256
VMEM_LIMIT = 48 * 1024 * 1024

SB_DEAD = 110.0
DF_DEAD = 112.0
DF_BOUND_SLACK = 1.01
KPOS_RADIX = 32
DF_SAFE = 60.0

_BF16 = jnp.bfloat16
_F32 = jnp.float32
_NT = (((1,), (1,)), ((), ()))


def _rms(xf, g):
    ms = jnp.mean(xf * xf, axis=-1, keepdims=True)
    return xf * lax.rsqrt(ms + EPS) * g


def _max_row_norm(x):
    xf = x.astype(_F32)
    return jnp.max(jnp.sqrt(jnp.sum(xf * xf, axis=1, keepdims=True)))


def _inproj_kernel(x_ref, g_ref, w_ref, o_ref, *, n_chunk):
    xn = _rms(x_ref[...], g_ref[...]).astype(_BF16)
    for n in range(IN_W // n_chunk):
        sl = slice(n * n_chunk, (n + 1) * n_chunk)
        o_ref[:, sl] = jnp.dot(xn, w_ref[:, sl], preferred_element_type=_F32).astype(_BF16)


def _in_proj(x2, g, w, tm):
    m = x2.shape[0]
    return pl.pallas_call(
        functools.partial(_inproj_kernel, n_chunk=512),
        out_shape=jax.ShapeDtypeStruct((m, IN_W), _BF16),
        grid=(m // tm,),
        in_specs=[
            pl.BlockSpec((tm, D_MODEL), lambda i: (i, 0)),
            pl.BlockSpec((1, D_MODEL), lambda i: (0, 0)),
            pl.BlockSpec((D_MODEL, IN_W), lambda i: (0, 0), pipeline_mode=pl.Buffered(1)),
        ],
        out_specs=pl.BlockSpec((tm, IN_W), lambda i: (i, 0)),
        compiler_params=pltpu.CompilerParams(
            dimension_semantics=("arbitrary",), vmem_limit_bytes=VMEM_LIMIT),
        name="in_proj",
    )(x2, g, w)


def _sb_kernel(q_ref, k_ref, v_ref, km_ref, vm_ref, o_ref, acc_ref, c_ref):
    t_blk = SB_T
    i = pl.program_id(2)

    row = lax.broadcasted_iota(jnp.int32, (t_blk, t_blk), 0)
    col = lax.broadcasted_iota(jnp.int32, (t_blk, t_blk), 1)
    tri = (row > col).astype(_BF16)

    def block(k_src, v_src, mask, tri_m, first):
        heads = [slice(hh * HEAD_W, (hh + 1) * HEAD_W) for hh in range(SB_PAIR)]

        def scores(hs):
            return lax.dot_general(q_ref[:, hs], k_src[:, hs], _NT, preferred_element_type=_F32) * SB_SCALE

        def logits(z):
            t = jnp.log(1.0 + jnp.exp2(jnp.abs(z) * (-LOG2E)))
            sp = jnp.maximum(z, 0.0) + t
            lsig = jnp.minimum(z, 0.0) - t
            if mask is not None:
                sp = jnp.where(mask, sp, 0.0)
            return sp, lsig

        def weights(sp, lsig):
            suffix = jnp.dot(sp.astype(_BF16), tri_m, preferred_element_type=_F32)
            a = jnp.exp(lsig - suffix)
            if mask is not None:
                a = jnp.where(mask, a, 0.0)
            return a.astype(_BF16)

        def accumulate(hs, a, sp):
            pv = jnp.dot(a, v_src[:, hs], preferred_element_type=_F32)
            csum = jnp.broadcast_to(jnp.sum(sp, axis=1, keepdims=True), (t_blk, LANES))
            if first:
                acc_ref[:, hs] = pv
                c_new = csum
            else:
                c_old = c_ref[:, hs]
                acc_ref[:, hs] += jnp.exp(-c_old) * pv
                c_new = c_old + csum
            c_ref[:, hs] = c_new
            return jnp.min(c_new)

        n = SB_PAIR
        z, sl, a, cmin = [None] * n, [None] * n, [None] * n, None
        for step in range(n + 3):
            if step - 3 >= 0:
                cm = accumulate(heads[step - 3], a[step - 3], sl[step - 3][0])
                cmin = cm if cmin is None else jnp.minimum(cmin, cm)
            if 0 <= step - 2 < n:
                a[step - 2] = weights(*sl[step - 2])
            if 0 <= step - 1 < n:
                sl[step - 1] = logits(z[step - 1])
            if step < n:
                z[step] = scores(heads[step])
        return cmin

    def diag_only(_):
        return block(k_ref.at[pl.ds(0, t_blk), :], v_ref.at[pl.ds(0, t_blk), :], col < row, tri, True)

    def diag_and_previous(_):
        k0 = pl.multiple_of((i - 1) * t_blk, t_blk)
        wrow = lax.broadcasted_iota(jnp.int32, (2 * t_blk, 2 * t_blk), 0)
        wcol = lax.broadcasted_iota(jnp.int32, (2 * t_blk, 2 * t_blk), 1)
        qrow = lax.broadcasted_iota(jnp.int32, (t_blk, 2 * t_blk), 0)
        kcol = lax.broadcasted_iota(jnp.int32, (t_blk, 2 * t_blk), 1)
        return block(k_ref.at[pl.ds(k0, 2 * t_blk), :], v_ref.at[pl.ds(k0, 2 * t_blk), :],
                     kcol < qrow + t_blk, (wrow > wcol).astype(_BF16), True)

    cmin0 = lax.cond(i == 0, diag_only, diag_and_previous, 0)

    def cond(carry):
        jj, cmin = carry
        return jnp.logical_and(jj < i, cmin < SB_DEAD)

    def body(carry):
        jj, _ = carry
        k0 = pl.multiple_of((i - 1 - jj) * t_blk, t_blk)
        cmin = block(k_ref.at[pl.ds(k0, t_blk), :], v_ref.at[pl.ds(k0, t_blk), :], None, tri, False)
        return jj + 1, cmin

    _, cmin = lax.while_loop(cond, body, (jnp.minimum(i, 1), cmin0))

    @pl.when(cmin < SB_DEAD)
    def _():
        mrow = lax.broadcasted_iota(jnp.int32, (META_PAD, META_PAD), 0)
        mcol = lax.broadcasted_iota(jnp.int32, (META_PAD, META_PAD), 1)
        valid = lax.broadcasted_iota(jnp.int32, (t_blk, META_PAD), 1) < N_META
        block(km_ref, vm_ref, valid, (mrow > mcol).astype(_BF16), False)

    o_ref[...] = acc_ref[...].astype(o_ref.dtype)


def _sb_attention(proj, meta_proj, b, s):
    t_blk = SB_T
    w = SB_PAIR * HEAD_W
    return pl.pallas_call(
        _sb_kernel,
        out_shape=jax.ShapeDtypeStruct((b, s, D_MODEL), _BF16),
        grid=(b, HEADS // SB_PAIR, s // t_blk),
        in_specs=[
            pl.BlockSpec((None, t_blk, w), lambda bi, h, i: (bi, i, SBQ // SB_PAIR + h)),
            pl.BlockSpec((None, s, w), lambda bi, h, i: (bi, 0, SBK // SB_PAIR + h)),
            pl.BlockSpec((None, s, w), lambda bi, h, i: (bi, 0, SBV // SB_PAIR + h)),
            pl.BlockSpec((META_PAD, w), lambda bi, h, i: (0, SBK // SB_PAIR + h)),
            pl.BlockSpec((META_PAD, w), lambda bi, h, i: (0, SBV // SB_PAIR + h)),
        ],
        out_specs=pl.BlockSpec((None, t_blk, w), lambda bi, h, i: (bi, i, h)),
        scratch_shapes=[
            pltpu.VMEM((t_blk, w), _F32),
            pltpu.VMEM((t_blk, w), _F32),
        ],
        compiler_params=pltpu.CompilerParams(
            dimension_semantics=("arbitrary", "arbitrary", "arbitrary"),
            vmem_limit_bytes=VMEM_LIMIT),
        name="sb_attn",
    )(proj, proj, proj, meta_proj, meta_proj)


def _df_kernel(slopes_ref, lam_ref, g_ref, q_ref, k_ref, v_ref, km_ref, vm_ref, o_ref,
               acc1, acc2, m1, m2, corr_ref, vext_ref, vmext_ref, kpos_ref, kn_ref):
    t_blk = DF_T
    h = pl.program_id(1)
    i = pl.program_id(2)
    slope = slopes_ref[h]

    @pl.when(i == 0)
    def _():
        vext_ref[:, :HEAD_W] = v_ref[...]
        vext_ref[:, HEAD_W:] = jnp.ones((vext_ref.shape[0], LANES), _BF16)
        vmext_ref[:, :HEAD_W] = vm_ref[...]
        vmext_ref[:, HEAD_W:] = jnp.ones((META_PAD, LANES), _BF16)
        row = lax.broadcasted_iota(jnp.int32, (t_blk, t_blk), 0)
        col = lax.broadcasted_iota(jnp.int32, (t_blk, t_blk), 1)
        visible = col < (row // CHUNK + 1) * CHUNK
        later = jnp.maximum(col - row, 0).astype(_F32)
        corr_ref[...] = jnp.where(visible, (-2.0 * slope) * later, -jnp.inf)
        c = lax.broadcasted_iota(jnp.int32, (2 * t_blk, LANES), 0)
        ln = lax.broadcasted_iota(jnp.int32, (2 * t_blk, LANES), 1)
        kpos = jnp.where(ln == 0, c // KPOS_RADIX, jnp.where(ln == 1, c % KPOS_RADIX, 0))
        kpos_ref[...] = kpos.astype(_F32).astype(_BF16)
        kn_ref[0] = jnp.maximum(_max_row_norm(k_ref[...]), _max_row_norm(km_ref[...]))

    q = q_ref[...]
    qs = q * DF_SCALE
    lane = lax.broadcasted_iota(jnp.int32, (t_blk, HEAD_W), 1)
    zero = jnp.zeros_like(q)
    q_lo = jnp.where(lane < DF_HALF, qs, zero)
    q_hi = jnp.where(lane >= DF_HALF, qs, zero)
    qx = jnp.where(lane == 0, slope * KPOS_RADIX, jnp.where(lane == 1, slope, 0.0)).astype(_BF16)
    q1 = jnp.concatenate([q_lo, qx], axis=1)
    q2 = jnp.concatenate([q_hi, qx], axis=1)
    score_bound = _max_row_norm(q) * kn_ref[0] * (DF_SCALE * DF_BOUND_SLACK)

    def tiled(stat, width):
        return jnp.concatenate([stat] * (width // LANES), axis=1)

    def tile_scores(qa, kt, width, n_valid, diag_lo):
        k_aug = jnp.concatenate([kt, kpos_ref[:width, :]], axis=1)
        y = lax.dot_general(qa, k_aug, _NT, preferred_element_type=_F32)
        if n_valid is not None:
            valid = lax.broadcasted_iota(jnp.int32, (t_blk, width), 1) < n_valid
            y = jnp.where(valid, y, -jnp.inf)
        if diag_lo == 0:
            y = y + corr_ref[...]
        elif diag_lo is not None:
            y = jnp.concatenate([y[:, :diag_lo], y[:, diag_lo:] + corr_ref[...]], axis=1)
        return y

    def gap(dist, m_min):
        return score_bound - slope * dist.astype(_F32) - m_min

    d0 = pl.multiple_of(i * t_blk, t_blk)
    t_pos = lax.broadcasted_iota(jnp.int32, (t_blk, LANES), 0).astype(_F32)

    kd_f = k_ref[pl.ds(d0, t_blk), :].astype(_F32)
    self1 = jnp.sum(q_lo.astype(_F32) * kd_f, axis=1, keepdims=True)
    self2 = jnp.sum(q_hi.astype(_F32) * kd_f, axis=1, keepdims=True)
    self_min = jnp.min(jnp.minimum(self1, self2))

    def fixed_sweep(_):
        m_min = self_min
        m1[...] = jnp.broadcast_to(self1, (t_blk, LANES)) + slope * t_pos
        m2[...] = jnp.broadcast_to(self2, (t_blk, LANES)) + slope * t_pos

        def alive(dist):
            return gap(dist, m_min) > -DF_DEAD

        def stream(kt, vt, width, n_valid, diag_lo, cb, first):
            for qa, acc, m in ((q1, acc1, m1), (q2, acc2, m2)):
                y = tile_scores(qa, kt, width, n_valid, diag_lo)
                p = jnp.exp(y - tiled(m[...] - cb, width))
                pv = jnp.dot(p.astype(_BF16), vt, preferred_element_type=_F32)
                if first:
                    acc[...] = pv
                else:
                    acc[...] += pv

        def pair(jj):
            k0 = pl.multiple_of((i - 2 - jj) * t_blk, t_blk)
            cb = -slope * ((jj + 2) * t_blk).astype(_F32)
            stream(k_ref[pl.ds(k0, 2 * t_blk), :], vext_ref[pl.ds(k0, 2 * t_blk), :],
                   2 * t_blk, None, None, cb, False)

        @pl.when(i % 2 == 0)
        def _():
            stream(k_ref[pl.ds(d0, t_blk), :], vext_ref[pl.ds(d0, t_blk), :],
                   t_blk, None, 0, 0.0, True)

        @pl.when(i % 2 == 1)
        def _():
            k0 = pl.multiple_of((i - 1) * t_blk, t_blk)
            stream(k_ref[pl.ds(k0, 2 * t_blk), :], vext_ref[pl.ds(k0, 2 * t_blk), :],
                   2 * t_blk, None, t_blk, -slope * t_blk, True)

        def qcond(jj):
            return jnp.logical_and(jj + 3 < i, alive(jj * t_blk + 1))

        def qbody(jj):
            pair(jj)
            pair(jj + 2)
            return jj + 4

        jj = lax.while_loop(qcond, qbody, i % 2)

        @pl.when(jnp.logical_and(jj + 1 < i, alive(jj * t_blk + 1)))
        def _():
            pair(jj)

        @pl.when(alive(i * t_blk + 1))
        def _():
            cb_meta = -slope * (i * t_blk + N_META).astype(_F32)
            stream(km_ref[...], vmext_ref[...], META_PAD, N_META, None, cb_meta, False)

        return 0

    def online_sweep(_):
        def update(y, vt, cb, acc, m, first):
            mx = jnp.broadcast_to(jnp.max(y, axis=1, keepdims=True), (t_blk, LANES)) + cb
            if first:
                m_new = mx
            else:
                m_old = m[...]
                m_new = jnp.maximum(m_old, mx)
            p = jnp.exp(y - tiled(m_new - cb, y.shape[1]))
            pv = jnp.dot(p.astype(_BF16), vt, preferred_element_type=_F32)
            if first:
                acc[...] = pv
            else:
                acc[...] = tiled(jnp.exp(m_old - m_new), pv.shape[1]) * acc[...] + pv
            m[...] = m_new
            return m_new

        def tile(kt, vt, width, n_valid, diag_lo, cb, first):
            m1_new = update(tile_scores(q1, kt, width, n_valid, diag_lo), vt, cb, acc1, m1, first)
            m2_new = update(tile_scores(q2, kt, width, n_valid, diag_lo), vt, cb, acc2, m2, first)
            return jnp.min(jnp.minimum(m1_new, m2_new) - slope * t_pos)

        m_min0 = tile(k_ref[pl.ds(d0, t_blk), :], vext_ref[pl.ds(d0, t_blk), :],
                      t_blk, None, 0, 0.0, True)

        def cond(carry):
            jj, m_min = carry
            return jnp.logical_and(jj < i, gap(jj * t_blk + 1, m_min) > -DF_DEAD)

        def body(carry):
            jj, _ = carry
            k0 = pl.multiple_of((i - 1 - jj) * t_blk, t_blk)
            cb = -slope * ((jj + 1) * t_blk).astype(_F32)
            m_min = tile(k_ref[pl.ds(k0, t_blk), :], vext_ref[pl.ds(k0, t_blk), :],
                         t_blk, None, None, cb, False)
            return jj + 1, m_min

        _, m_min = lax.while_loop(cond, body, (jnp.int32(0), m_min0))

        @pl.when(gap(i * t_blk + 1, m_min) > -DF_DEAD)
        def _():
            cb_meta = -slope * (i * t_blk + N_META).astype(_F32)
            tile(km_ref[...], vmext_ref[...], META_PAD, N_META, None, cb_meta, False)

        return 0

    lax.cond(score_bound - self_min < DF_SAFE, fixed_sweep, online_sweep, 0)

    lv = lam_ref[...]
    d1 = jnp.sum(lv[0:1, :] * lv[1:2, :], axis=1, keepdims=True)
    d2 = jnp.sum(lv[2:3, :] * lv[3:4, :], axis=1, keepdims=True)
    lam = jnp.exp(d1) - jnp.exp(d2) + LAM_INIT
    a1 = acc1[...]
    a2 = acc2[...]
    o = a1[:, :HEAD_W] * (1.0 / a1[:, HEAD_W:]) - lam * (a2[:, :HEAD_W] * (1.0 / a2[:, HEAD_W:]))
    o_ref[...] = (_rms(o, g_ref[...]) * (1.0 - LAM_INIT)).astype(o_ref.dtype)


def _df_attention(proj, meta_proj, slopes, lamvec, subln_g, b, s):
    t_blk = DF_T
    return pl.pallas_call(
        _df_kernel,
        out_shape=jax.ShapeDtypeStruct((b, s, D_MODEL), _BF16),
        grid=(b, HEADS, s // t_blk),
        in_specs=[
            pl.BlockSpec(memory_space=pltpu.SMEM),
            pl.BlockSpec((4, DF_HALF), lambda bi, h, i: (0, 0)),
            pl.BlockSpec((1, HEAD_W), lambda bi, h, i: (0, 0)),
            pl.BlockSpec((None, t_blk, HEAD_W), lambda bi, h, i: (bi, i, DFQ + h)),
            pl.BlockSpec((None, s, HEAD_W), lambda bi, h, i: (bi, 0, DFK + h)),
            pl.BlockSpec((None, s, HEAD_W), lambda bi, h, i: (bi, 0, DFV + h)),
            pl.BlockSpec((META_PAD, HEAD_W), lambda bi, h, i: (0, DFK + h)),
            pl.BlockSpec((META_PAD, HEAD_W), lambda bi, h, i: (0, DFV + h)),
        ],
        out_specs=pl.BlockSpec((None, t_blk, HEAD_W), lambda bi, h, i: (bi, i, h)),
        scratch_shapes=[
            pltpu.VMEM((t_blk, HEAD_W + LANES), _F32),
            pltpu.VMEM((t_blk, HEAD_W + LANES), _F32),
            pltpu.VMEM((t_blk, LANES), _F32),
            pltpu.VMEM((t_blk, LANES), _F32),
            pltpu.VMEM((t_blk, t_blk), _F32),
            pltpu.VMEM((s, HEAD_W + LANES), _BF16),
            pltpu.VMEM((META_PAD, HEAD_W + LANES), _BF16),
            pltpu.VMEM((2 * t_blk, LANES), _BF16),
            pltpu.SMEM((1,), _F32),
        ],
        compiler_params=pltpu.CompilerParams(
            dimension_semantics=("arbitrary", "arbitrary", "arbitrary"),
            vmem_limit_bytes=VMEM_LIMIT),
        name="df_attn",
    )(slopes, lamvec, subln_g, proj, proj, proj, meta_proj, meta_proj)


def _merge_kernel(x_ref, g_ref, osb_ref, odf_ref, wg_ref, bg_ref, wsb_ref, wdf_ref, wo_ref,
                  h_ref, merged_ref, *, n_chunk):
    x = x_ref[...]
    xn = _rms(x, g_ref[...]).astype(_BF16)
    osb = osb_ref[...]
    odf = odf_ref[...]
    for n in range(D_MODEL // n_chunk):
        sa = slice(n * n_chunk, (n + 1) * n_chunk)
        sb = slice(D_MODEL + n * n_chunk, D_MODEL + (n + 1) * n_chunk)
        ga = jax.nn.sigmoid(jnp.dot(xn, wg_ref[:, sa], preferred_element_type=_F32) + bg_ref[:, sa])
        gb = jax.nn.sigmoid(jnp.dot(xn, wg_ref[:, sb], preferred_element_type=_F32) + bg_ref[:, sb])
        ysb = jnp.dot(osb, wsb_ref[:, sa], preferred_element_type=_F32)
        ydf = jnp.dot(odf, wdf_ref[:, sa], preferred_element_type=_F32)
        merged_ref[:, sa] = (ga * ysb + gb * ydf).astype(_BF16)
    h_ref[...] = x + jnp.dot(merged_ref[...], wo_ref[...], preferred_element_type=_F32)


def _merge(x2, g, osb, odf, wg, bg, wsb, wdf, wo):
    m = x2.shape[0]
    const = lambda shape: pl.BlockSpec(shape, lambda i: (0, 0), pipeline_mode=pl.Buffered(1))
    rows = lambda: pl.BlockSpec((TM, D_MODEL), lambda i: (i, 0))
    return pl.pallas_call(
        functools.partial(_merge_kernel, n_chunk=512),
        out_shape=jax.ShapeDtypeStruct((m, D_MODEL), _F32),
        grid=(m // TM,),
        in_specs=[
            rows(), const((1, D_MODEL)), rows(), rows(),
            const((D_MODEL, 2 * D_MODEL)), const((1, 2 * D_MODEL)),
            const((D_MODEL, D_MODEL)), const((D_MODEL, D_MODEL)), const((D_MODEL, D_MODEL)),
        ],
        out_specs=rows(),
        scratch_shapes=[pltpu.VMEM((TM, D_MODEL), _BF16)],
        compiler_params=pltpu.CompilerParams(
            dimension_semantics=("arbitrary",), vmem_limit_bytes=VMEM_LIMIT),
        name="merge",
    )(x2, g, osb, odf, wg, bg, wsb, wdf, wo)


def _ffn_kernel(h_ref, gf_ref, wg_ref, wu_ref, wd_ref, gl_ref, y_ref, acc_ref):
    h = h_ref[...]
    hn = _rms(h, gf_ref[...]).astype(_BF16)
    for n in range(D_FF // FF_CHUNK):
        sl = slice(n * FF_CHUNK, (n + 1) * FF_CHUNK)
        gate = jnp.dot(hn, wg_ref[:, sl], preferred_element_type=_F32)
        up = jnp.dot(hn, wu_ref[:, sl], preferred_element_type=_F32)
        act = (jax.nn.silu(gate) * up).astype(_BF16)
        down = jnp.dot(act, wd_ref[sl, :], preferred_element_type=_F32)
        if n == 0:
            acc_ref[...] = down
        else:
            acc_ref[...] += down
    y_ref[...] = _rms(h + acc_ref[...], gl_ref[...])


def _ffn(h1, gf, wg, wu, wd, gl):
    m = h1.shape[0]
    const = lambda shape: pl.BlockSpec(shape, lambda i: (0, 0), pipeline_mode=pl.Buffered(1))
    rows = lambda: pl.BlockSpec((TM, D_MODEL), lambda i: (i, 0))
    return pl.pallas_call(
        _ffn_kernel,
        out_shape=jax.ShapeDtypeStruct((m, D_MODEL), _F32),
        grid=(m // TM,),
        in_specs=[
            rows(), const((1, D_MODEL)),
            const((D_MODEL, D_FF)), const((D_MODEL, D_FF)), const((D_FF, D_MODEL)),
            const((1, D_MODEL)),
        ],
        out_specs=rows(),
        scratch_shapes=[pltpu.VMEM((TM, D_MODEL), _F32)],
        compiler_params=pltpu.CompilerParams(
            dimension_semantics=("arbitrary",), vmem_limit_bytes=VMEM_LIMIT),
        name="ffn",
    )(h1, gf, wg, wu, wd, gl)


def kernel(x, meta, norm_mix_g, w_in, w_gate, b_gate, lam_q1, lam_k1, lam_q2, lam_k2, subln_g,
           w_br_sb, w_br_df, w_out, norm_ffn_g, w_ffn_gate, w_ffn_up, w_ffn_down, norm_final_g):
    b, s, d = x.shape
    assert d == D_MODEL and s % SB_T == 0 and s % DF_T == 0 and (b * s) % TM == 0
    assert meta.shape == (N_META, D_MODEL) and w_in.shape[0] == 1

    bf = lambda w: w.astype(_BF16)
    x2 = x.reshape(b * s, d)
    g_mix = norm_mix_g[0][None, :]
    w_in_b = bf(w_in[0])

    proj = _in_proj(x2, g_mix, w_in_b, TM).reshape(b, s, IN_W)
    meta_proj = _in_proj(meta.astype(x.dtype), g_mix, w_in_b, N_META)
    meta_proj = jnp.pad(meta_proj, ((0, META_PAD - N_META), (0, 0)))

    slopes = jnp.exp2(-8.0 * (jnp.arange(HEADS) + 1) / HEADS).astype(_F32)
    lamvec = jnp.stack([lam_q1[0], lam_k1[0], lam_q2[0], lam_k2[0]]).astype(_F32)

    o_sb = _sb_attention(proj, meta_proj, b, s).reshape(b * s, d)
    o_df = _df_attention(proj, meta_proj, slopes, lamvec, subln_g[0][None, :], b, s).reshape(b * s, d)

    h1 = _merge(x2, g_mix, o_sb, o_df, bf(w_gate[0]), b_gate[0][None, :],
                bf(w_br_sb[0]), bf(w_br_df[0]), bf(w_out[0]))
    y = _ffn(h1, norm_ffn_g[0][None, :], bf(w_ffn_gate[0]), bf(w_ffn_up[0]), bf(w_ffn_down[0]),
             norm_final_g[None, :])
    return y.reshape(b, s, d)
```

```python
import functools
import math

import jax
import jax.numpy as jnp
from jax import lax
from jax.experimental import pallas as pl
from jax.experimental.pallas import tpu as pltpu

D_MODEL = 1024
N_META = 16
CHUNK = 64
HEADS = 8
HEAD_W = 128
DF_HALF = 64
IN_W = 6 * D_MODEL
D_FF = 2816
EPS = 1e-6
LAM_INIT = 0.8 - 0.6 * math.exp(-0.3 * 0)
LOG2E = 1.4426950408889634
SB_SCALE = HEAD_W ** -0.5
DF_SCALE = DF_HALF ** -0.5

SBQ, SBK, SBV, DFQ, DFK, DFV = 0, 8, 16, 24, 32, 40

LANES = 128
META_PAD = LANES
SB_T = 256
SB_PAIR = 4
DF_T = 512
TM = 512
FF_CHUNK = 256
VMEM_LIMIT = 48 * 1024 * 1024

SB_DEAD = 110.0
DF_DEAD = 112.0
DF_BOUND_SLACK = 1.01
KPOS_RADIX = 32
KT_CHUNK = 1024
DF_SAFE = 60.0

_BF16 = jnp.bfloat16
_F32 = jnp.float32
_NT = (((1,), (1,)), ((), ()))


def _rms(xf, g):
    ms = jnp.mean(xf * xf, axis=-1, keepdims=True)
    return xf * lax.rsqrt(ms + EPS) * g


def _max_row_norm(x):
    xf = x.astype(_F32)
    return jnp.max(jnp.sqrt(jnp.sum(xf * xf, axis=1, keepdims=True)))


def _inproj_kernel(x_ref, g_ref, w_ref, o_ref, *, n_chunk):
    xn = _rms(x_ref[...], g_ref[...]).astype(_BF16)
    for n in range(IN_W // n_chunk):
        sl = slice(n * n_chunk, (n + 1) * n_chunk)
        o_ref[:, sl] = jnp.dot(xn, w_ref[:, sl], preferred_element_type=_F32).astype(_BF16)


def _in_proj(x2, g, w, tm):
    m = x2.shape[0]
    return pl.pallas_call(
        functools.partial(_inproj_kernel, n_chunk=512),
        out_shape=jax.ShapeDtypeStruct((m, IN_W), _BF16),
        grid=(m // tm,),
        in_specs=[
            pl.BlockSpec((tm, D_MODEL), lambda i: (i, 0)),
            pl.BlockSpec((1, D_MODEL), lambda i: (0, 0)),
            pl.BlockSpec((D_MODEL, IN_W), lambda i: (0, 0), pipeline_mode=pl.Buffered(1)),
        ],
        out_specs=pl.BlockSpec((tm, IN_W), lambda i: (i, 0)),
        compiler_params=pltpu.CompilerParams(
            dimension_semantics=("arbitrary",), vmem_limit_bytes=VMEM_LIMIT),
        name="in_proj",
    )(x2, g, w)


def _sb_kernel(q_ref, k_ref, v_ref, km_ref, vm_ref, o_ref, acc_ref, c_ref):
    t_blk = SB_T
    i = pl.program_id(2)

    row = lax.broadcasted_iota(jnp.int32, (t_blk, t_blk), 0)
    col = lax.broadcasted_iota(jnp.int32, (t_blk, t_blk), 1)
    tri = (row > col).astype(_BF16)

    def block(k_src, v_src, mask, tri_m, first):
        heads = [slice(hh * HEAD_W, (hh + 1) * HEAD_W) for hh in range(SB_PAIR)]

        def scores(hs):
            return lax.dot_general(q_ref[:, hs], k_src[:, hs], _NT, preferred_element_type=_F32) * SB_SCALE

        def logits(z):
            t = jnp.log(1.0 + jnp.exp2(jnp.abs(z) * (-LOG2E)))
            sp = jnp.maximum(z, 0.0) + t
            lsig = jnp.minimum(z, 0.0) - t
            if mask is not None:
                sp = jnp.where(mask, sp, 0.0)
            return sp, lsig

        def weights(sp, lsig):
            suffix = jnp.dot(sp.astype(_BF16), tri_m, preferred_element_type=_F32)
            a = jnp.exp(lsig - suffix)
            if mask is not None:
                a = jnp.where(mask, a, 0.0)
            return a.astype(_BF16)

        def accumulate(hs, a, sp):
            pv = jnp.dot(a, v_src[:, hs], preferred_element_type=_F32)
            csum = jnp.broadcast_to(jnp.sum(sp, axis=1, keepdims=True), (t_blk, LANES))
            if first:
                acc_ref[:, hs] = pv
                c_new = csum
            else:
                c_old = c_ref[:, hs]
                acc_ref[:, hs] += jnp.exp(-c_old) * pv
                c_new = c_old + csum
            c_ref[:, hs] = c_new
            return jnp.min(c_new)

        n = SB_PAIR
        z, sl, a, cmin = [None] * n, [None] * n, [None] * n, None
        for step in range(n + 3):
            if step - 3 >= 0:
                cm = accumulate(heads[step - 3], a[step - 3], sl[step - 3][0])
                cmin = cm if cmin is None else jnp.minimum(cmin, cm)
            if 0 <= step - 2 < n:
                a[step - 2] = weights(*sl[step - 2])
            if 0 <= step - 1 < n:
                sl[step - 1] = logits(z[step - 1])
            if step < n:
                z[step] = scores(heads[step])
        return cmin

    def diag_only(_):
        return block(k_ref.at[pl.ds(0, t_blk), :], v_ref.at[pl.ds(0, t_blk), :], col < row, tri, True)

    def diag_and_previous(_):
        k0 = pl.multiple_of((i - 1) * t_blk, t_blk)
        wrow = lax.broadcasted_iota(jnp.int32, (2 * t_blk, 2 * t_blk), 0)
        wcol = lax.broadcasted_iota(jnp.int32, (2 * t_blk, 2 * t_blk), 1)
        qrow = lax.broadcasted_iota(jnp.int32, (t_blk, 2 * t_blk), 0)
        kcol = lax.broadcasted_iota(jnp.int32, (t_blk, 2 * t_blk), 1)
        return block(k_ref.at[pl.ds(k0, 2 * t_blk), :], v_ref.at[pl.ds(k0, 2 * t_blk), :],
                     kcol < qrow + t_blk, (wrow > wcol).astype(_BF16), True)

    cmin0 = lax.cond(i == 0, diag_only, diag_and_previous, 0)

    def cond(carry):
        jj, cmin = carry
        return jnp.logical_and(jj < i, cmin < SB_DEAD)

    def body(carry):
        jj, _ = carry
        k0 = pl.multiple_of((i - 1 - jj) * t_blk, t_blk)
        cmin = block(k_ref.at[pl.ds(k0, t_blk), :], v_ref.at[pl.ds(k0, t_blk), :], None, tri, False)
        return jj + 1, cmin

    _, cmin = lax.while_loop(cond, body, (jnp.minimum(i, 1), cmin0))

    @pl.when(cmin < SB_DEAD)
    def _():
        mrow = lax.broadcasted_iota(jnp.int32, (META_PAD, META_PAD), 0)
        mcol = lax.broadcasted_iota(jnp.int32, (META_PAD, META_PAD), 1)
        valid = lax.broadcasted_iota(jnp.int32, (t_blk, META_PAD), 1) < N_META
        block(km_ref, vm_ref, valid, (mrow > mcol).astype(_BF16), False)

    o_ref[...] = acc_ref[...].astype(o_ref.dtype)


def _sb_attention(proj, meta_proj, b, s):
    t_blk = SB_T
    w = SB_PAIR * HEAD_W
    return pl.pallas_call(
        _sb_kernel,
        out_shape=jax.ShapeDtypeStruct((b, s, D_MODEL), _BF16),
        grid=(b, HEADS // SB_PAIR, s // t_blk),
        in_specs=[
            pl.BlockSpec((None, t_blk, w), lambda bi, h, i: (bi, i, SBQ // SB_PAIR + h)),
            pl.BlockSpec((None, s, w), lambda bi, h, i: (bi, 0, SBK // SB_PAIR + h)),
            pl.BlockSpec((None, s, w), lambda bi, h, i: (bi, 0, SBV // SB_PAIR + h)),
            pl.BlockSpec((META_PAD, w), lambda bi, h, i: (0, SBK // SB_PAIR + h)),
            pl.BlockSpec((META_PAD, w), lambda bi, h, i: (0, SBV // SB_PAIR + h)),
        ],
        out_specs=pl.BlockSpec((None, t_blk, w), lambda bi, h, i: (bi, i, h)),
        scratch_shapes=[
            pltpu.VMEM((t_blk, w), _F32),
            pltpu.VMEM((t_blk, w), _F32),
        ],
        compiler_params=pltpu.CompilerParams(
            dimension_semantics=("arbitrary", "arbitrary", "arbitrary"),
            vmem_limit_bytes=VMEM_LIMIT),
        name="sb_attn",
    )(proj, proj, proj, meta_proj, meta_proj)


def _df_kernel(slopes_ref, lam_ref, g_ref, q_ref, k_ref, v_ref, km_ref, vm_ref, o_ref,
               acc1, acc2, m1, m2, corr_ref, vext_ref, vmext_ref, kT_ref, kmT_ref, kn_ref):
    t_blk = DF_T
    h = pl.program_id(1)
    i = pl.program_id(2)
    slope = slopes_ref[h]

    @pl.when(i == 0)
    def _():
        vext_ref[:, :HEAD_W] = v_ref[...]
        vext_ref[:, HEAD_W:] = jnp.ones((vext_ref.shape[0], LANES), _BF16)
        vmext_ref[:, :HEAD_W] = vm_ref[...]
        vmext_ref[:, HEAD_W:] = jnp.ones((META_PAD, LANES), _BF16)
        row = lax.broadcasted_iota(jnp.int32, (t_blk, t_blk), 0)
        col = lax.broadcasted_iota(jnp.int32, (t_blk, t_blk), 1)
        visible = col < (row // CHUNK + 1) * CHUNK
        later = jnp.maximum(col - row, 0).astype(_F32)
        corr_ref[...] = jnp.where(visible, (-2.0 * slope) * later, -jnp.inf)
        def digits(g):
            rw = lax.broadcasted_iota(jnp.int32, g.shape, 0)
            d = jnp.where(rw == 0, g // (KPOS_RADIX * KPOS_RADIX),
                          jnp.where(rw == 1, (g // KPOS_RADIX) % KPOS_RADIX,
                                    jnp.where(rw == 2, g % KPOS_RADIX, 0)))
            return d.astype(_F32).astype(_BF16)

        for c in range(kT_ref.shape[1] // KT_CHUNK):
            cs = slice(c * KT_CHUNK, (c + 1) * KT_CHUNK)
            kT_ref[:HEAD_W, cs] = k_ref[cs, :].T
            kT_ref[HEAD_W:, cs] = digits(lax.broadcasted_iota(jnp.int32, (LANES, KT_CHUNK), 1) + c * KT_CHUNK)
        kmT_ref[:HEAD_W, :] = km_ref[...].T
        kmT_ref[HEAD_W:, :] = digits(lax.broadcasted_iota(jnp.int32, (LANES, META_PAD), 1))
        kn_ref[0] = jnp.maximum(_max_row_norm(k_ref[...]), _max_row_norm(km_ref[...]))

    q = q_ref[...]
    qs = q * DF_SCALE
    lane = lax.broadcasted_iota(jnp.int32, (t_blk, HEAD_W), 1)
    zero = jnp.zeros_like(q)
    q_lo = jnp.where(lane < DF_HALF, qs, zero)
    q_hi = jnp.where(lane >= DF_HALF, qs, zero)
    qx = jnp.where(lane == 0, slope * (KPOS_RADIX * KPOS_RADIX),
                   jnp.where(lane == 1, slope * KPOS_RADIX,
                             jnp.where(lane == 2, slope, 0.0))).astype(_BF16)
    q1 = jnp.concatenate([q_lo, qx], axis=1)
    q2 = jnp.concatenate([q_hi, qx], axis=1)
    score_bound = _max_row_norm(q) * kn_ref[0] * (DF_SCALE * DF_BOUND_SLACK)

    def tiled(stat, width):
        return jnp.concatenate([stat] * (width // LANES), axis=1)

    def tile_scores(qa, kt, width, n_valid, diag_lo):
        y = jnp.dot(qa, kt, preferred_element_type=_F32)
        if n_valid is not None:
            valid = lax.broadcasted_iota(jnp.int32, (t_blk, width), 1) < n_valid
            y = jnp.where(valid, y, -jnp.inf)
        if diag_lo == 0:
            y = y + corr_ref[...]
        elif diag_lo is not None:
            y = jnp.concatenate([y[:, :diag_lo], y[:, diag_lo:] + corr_ref[...]], axis=1)
        return y

    def gap(dist, m_min):
        return score_bound - slope * dist.astype(_F32) - m_min

    d0 = pl.multiple_of(i * t_blk, t_blk)
    t_pos = lax.broadcasted_iota(jnp.int32, (t_blk, LANES), 0).astype(_F32)
    cb_real = -slope * (i * t_blk).astype(_F32)

    kd_f = k_ref[pl.ds(d0, t_blk), :].astype(_F32)
    self1 = jnp.sum(q_lo.astype(_F32) * kd_f, axis=1, keepdims=True)
    self2 = jnp.sum(q_hi.astype(_F32) * kd_f, axis=1, keepdims=True)
    self_min = jnp.min(jnp.minimum(self1, self2))

    def fixed_sweep(_):
        m_min = self_min
        m1[...] = jnp.broadcast_to(self1, (t_blk, LANES)) + slope * t_pos
        m2[...] = jnp.broadcast_to(self2, (t_blk, LANES)) + slope * t_pos

        def alive(dist):
            return gap(dist, m_min) > -DF_DEAD

        def stream(kt, vt, width, n_valid, diag_lo, cb, first):
            for qa, acc, m in ((q1, acc1, m1), (q2, acc2, m2)):
                y = tile_scores(qa, kt, width, n_valid, diag_lo)
                p = jnp.exp((y - tiled(m[...] - cb, width)).astype(_BF16))
                pv = jnp.dot(p, vt, preferred_element_type=_F32)
                if first:
                    acc[...] = pv
                else:
                    acc[...] += pv

        def pair(jj):
            k0 = pl.multiple_of((i - 2 - jj) * t_blk, t_blk)
            stream(kT_ref[:, pl.ds(k0, 2 * t_blk)], vext_ref[pl.ds(k0, 2 * t_blk), :],
                   2 * t_blk, None, None, cb_real, False)

        @pl.when(i % 2 == 0)
        def _():
            stream(kT_ref[:, pl.ds(d0, t_blk)], vext_ref[pl.ds(d0, t_blk), :],
                   t_blk, None, 0, cb_real, True)

        @pl.when(i % 2 == 1)
        def _():
            k0 = pl.multiple_of((i - 1) * t_blk, t_blk)
            stream(kT_ref[:, pl.ds(k0, 2 * t_blk)], vext_ref[pl.ds(k0, 2 * t_blk), :],
                   2 * t_blk, None, t_blk, cb_real, True)

        def qcond(jj):
            return jnp.logical_and(jj + 3 < i, alive(jj * t_blk + 1))

        def qbody(jj):
            pair(jj)
            pair(jj + 2)
            return jj + 4

        jj = lax.while_loop(qcond, qbody, i % 2)

        @pl.when(jnp.logical_and(jj + 1 < i, alive(jj * t_blk + 1)))
        def _():
            pair(jj)

        @pl.when(alive(i * t_blk + 1))
        def _():
            cb_meta = -slope * (i * t_blk + N_META).astype(_F32)
            stream(kmT_ref[...], vmext_ref[...], META_PAD, N_META, None, cb_meta, False)

        return 0

    def online_sweep(_):
        def update(y, vt, cb, acc, m, first):
            mx = jnp.broadcast_to(jnp.max(y, axis=1, keepdims=True), (t_blk, LANES)) + cb
            if first:
                m_new = mx
            else:
                m_old = m[...]
                m_new = jnp.maximum(m_old, mx)
            p = jnp.exp(y - tiled(m_new - cb, y.shape[1]))
            pv = jnp.dot(p.astype(_BF16), vt, preferred_element_type=_F32)
            if first:
                acc[...] = pv
            else:
                acc[...] = tiled(jnp.exp(m_old - m_new), pv.shape[1]) * acc[...] + pv
            m[...] = m_new
            return m_new

        def tile(kt, vt, width, n_valid, diag_lo, cb, first):
            m1_new = update(tile_scores(q1, kt, width, n_valid, diag_lo), vt, cb, acc1, m1, first)
            m2_new = update(tile_scores(q2, kt, width, n_valid, diag_lo), vt, cb, acc2, m2, first)
            return jnp.min(jnp.minimum(m1_new, m2_new) - slope * t_pos)

        m_min0 = tile(kT_ref[:, pl.ds(d0, t_blk)], vext_ref[pl.ds(d0, t_blk), :],
                      t_blk, None, 0, cb_real, True)

        def cond(carry):
            jj, m_min = carry
            return jnp.logical_and(jj < i, gap(jj * t_blk + 1, m_min) > -DF_DEAD)

        def body(carry):
            jj, _ = carry
            k0 = pl.multiple_of((i - 1 - jj) * t_blk, t_blk)
            m_min = tile(kT_ref[:, pl.ds(k0, t_blk)], vext_ref[pl.ds(k0, t_blk), :],
                         t_blk, None, None, cb_real, False)
            return jj + 1, m_min

        _, m_min = lax.while_loop(cond, body, (jnp.int32(0), m_min0))

        @pl.when(gap(i * t_blk + 1, m_min) > -DF_DEAD)
        def _():
            cb_meta = -slope * (i * t_blk + N_META).astype(_F32)
            tile(kmT_ref[...], vmext_ref[...], META_PAD, N_META, None, cb_meta, False)

        return 0

    lax.cond(score_bound - self_min < DF_SAFE, fixed_sweep, online_sweep, 0)

    lv = lam_ref[...]
    d1 = jnp.sum(lv[0:1, :] * lv[1:2, :], axis=1, keepdims=True)
    d2 = jnp.sum(lv[2:3, :] * lv[3:4, :], axis=1, keepdims=True)
    lam = jnp.exp(d1) - jnp.exp(d2) + LAM_INIT
    a1 = acc1[...]
    a2 = acc2[...]
    o = a1[:, :HEAD_W] * (1.0 / a1[:, HEAD_W:]) - lam * (a2[:, :HEAD_W] * (1.0 / a2[:, HEAD_W:]))
    o_ref[...] = (_rms(o, g_ref[...]) * (1.0 - LAM_INIT)).astype(o_ref.dtype)


def _df_attention(proj, meta_proj, slopes, lamvec, subln_g, b, s):
    t_blk = DF_T
    return pl.pallas_call(
        _df_kernel,
        out_shape=jax.ShapeDtypeStruct((b, s, D_MODEL), _BF16),
        grid=(b, HEADS, s // t_blk),
        in_specs=[
            pl.BlockSpec(memory_space=pltpu.SMEM),
            pl.BlockSpec((4, DF_HALF), lambda bi, h, i: (0, 0)),
            pl.BlockSpec((1, HEAD_W), lambda bi, h, i: (0, 0)),
            pl.BlockSpec((None, t_blk, HEAD_W), lambda bi, h, i: (bi, i, DFQ + h)),
            pl.BlockSpec((None, s, HEAD_W), lambda bi, h, i: (bi, 0, DFK + h)),
            pl.BlockSpec((None, s, HEAD_W), lambda bi, h, i: (bi, 0, DFV + h)),
            pl.BlockSpec((META_PAD, HEAD_W), lambda bi, h, i: (0, DFK + h)),
            pl.BlockSpec((META_PAD, HEAD_W), lambda bi, h, i: (0, DFV + h)),
        ],
        out_specs=pl.BlockSpec((None, t_blk, HEAD_W), lambda bi, h, i: (bi, i, h)),
        scratch_shapes=[
            pltpu.VMEM((t_blk, HEAD_W + LANES), _F32),
            pltpu.VMEM((t_blk, HEAD_W + LANES), _F32),
            pltpu.VMEM((t_blk, LANES), _F32),
            pltpu.VMEM((t_blk, LANES), _F32),
            pltpu.VMEM((t_blk, t_blk), _F32),
            pltpu.VMEM((s, HEAD_W + LANES), _BF16),
            pltpu.VMEM((META_PAD, HEAD_W + LANES), _BF16),
            pltpu.VMEM((HEAD_W + LANES, s), _BF16),
            pltpu.VMEM((HEAD_W + LANES, META_PAD), _BF16),
            pltpu.SMEM((1,), _F32),
        ],
        compiler_params=pltpu.CompilerParams(
            dimension_semantics=("arbitrary", "arbitrary", "arbitrary"),
            vmem_limit_bytes=VMEM_LIMIT),
        name="df_attn",
    )(slopes, lamvec, subln_g, proj, proj, proj, meta_proj, meta_proj)


def _merge_kernel(x_ref, g_ref, osb_ref, odf_ref, wg_ref, bg_ref, wsb_ref, wdf_ref, wo_ref,
                  h_ref, merged_ref, *, n_chunk):
    x = x_ref[...]
    xn = _rms(x, g_ref[...]).astype(_BF16)
    osb = osb_ref[...]
    odf = odf_ref[...]
    for n in range(D_MODEL // n_chunk):
        sa = slice(n * n_chunk, (n + 1) * n_chunk)
        sb = slice(D_MODEL + n * n_chunk, D_MODEL + (n + 1) * n_chunk)
        ga = jax.nn.sigmoid(jnp.dot(xn, wg_ref[:, sa], preferred_element_type=_F32) + bg_ref[:, sa])
        gb = jax.nn.sigmoid(jnp.dot(xn, wg_ref[:, sb], preferred_element_type=_F32) + bg_ref[:, sb])
        ysb = jnp.dot(osb, wsb_ref[:, sa], preferred_element_type=_F32)
        ydf = jnp.dot(odf, wdf_ref[:, sa], preferred_element_type=_F32)
        merged_ref[:, sa] = (ga * ysb + gb * ydf).astype(_BF16)
    h_ref[...] = x + jnp.dot(merged_ref[...], wo_ref[...], preferred_element_type=_F32)


def _merge(x2, g, osb, odf, wg, bg, wsb, wdf, wo):
    m = x2.shape[0]
    const = lambda shape: pl.BlockSpec(shape, lambda i: (0, 0), pipeline_mode=pl.Buffered(1))
    rows = lambda: pl.BlockSpec((TM, D_MODEL), lambda i: (i, 0))
    return pl.pallas_call(
        functools.partial(_merge_kernel, n_chunk=512),
        out_shape=jax.ShapeDtypeStruct((m, D_MODEL), _F32),
        grid=(m // TM,),
        in_specs=[
            rows(), const((1, D_MODEL)), rows(), rows(),
            const((D_MODEL, 2 * D_MODEL)), const((1, 2 * D_MODEL)),
            const((D_MODEL, D_MODEL)), const((D_MODEL, D_MODEL)), const((D_MODEL, D_MODEL)),
        ],
        out_specs=rows(),
        scratch_shapes=[pltpu.VMEM((TM, D_MODEL), _BF16)],
        compiler_params=pltpu.CompilerParams(
            dimension_semantics=("arbitrary",), vmem_limit_bytes=VMEM_LIMIT),
        name="merge",
    )(x2, g, osb, odf, wg, bg, wsb, wdf, wo)


def _ffn_kernel(h_ref, gf_ref, wg_ref, wu_ref, wd_ref, gl_ref, y_ref, acc_ref):
    h = h_ref[...]
    hn = _rms(h, gf_ref[...]).astype(_BF16)
    for n in range(D_FF // FF_CHUNK):
        sl = slice(n * FF_CHUNK, (n + 1) * FF_CHUNK)
        gate = jnp.dot(hn, wg_ref[:, sl], preferred_element_type=_F32)
        up = jnp.dot(hn, wu_ref[:, sl], preferred_element_type=_F32)
        act = (jax.nn.silu(gate) * up).astype(_BF16)
        down = jnp.dot(act, wd_ref[sl, :], preferred_element_type=_F32)
        if n == 0:
            acc_ref[...] = down
        else:
            acc_ref[...] += down
    y_ref[...] = _rms(h + acc_ref[...], gl_ref[...])


def _ffn(h1, gf, wg, wu, wd, gl):
    m = h1.shape[0]
    const = lambda shape: pl.BlockSpec(shape, lambda i: (0, 0), pipeline_mode=pl.Buffered(1))
    rows = lambda: pl.BlockSpec((TM, D_MODEL), lambda i: (i, 0))
    return pl.pallas_call(
        _ffn_kernel,
        out_shape=jax.ShapeDtypeStruct((m, D_MODEL), _F32),
        grid=(m // TM,),
        in_specs=[
            rows(), const((1, D_MODEL)),
            const((D_MODEL, D_FF)), const((D_MODEL, D_FF)), const((D_FF, D_MODEL)),
            const((1, D_MODEL)),
        ],
        out_specs=rows(),
        scratch_shapes=[pltpu.VMEM((TM, D_MODEL), _F32)],
        compiler_params=pltpu.CompilerParams(
            dimension_semantics=("arbitrary",), vmem_limit_bytes=VMEM_LIMIT),
        name="ffn",
    )(h1, gf, wg, wu, wd, gl)


def kernel(x, meta, norm_mix_g, w_in, w_gate, b_gate, lam_q1, lam_k1, lam_q2, lam_k2, subln_g,
           w_br_sb, w_br_df, w_out, norm_ffn_g, w_ffn_gate, w_ffn_up, w_ffn_down, norm_final_g):
    b, s, d = x.shape
    assert d == D_MODEL and s % SB_T == 0 and s % DF_T == 0 and (b * s) % TM == 0
    assert meta.shape == (N_META, D_MODEL) and w_in.shape[0] == 1

    bf = lambda w: w.astype(_BF16)
    x2 = x.reshape(b * s, d)
    g_mix = norm_mix_g[0][None, :]
    w_in_b = bf(w_in[0])

    proj = _in_proj(x2, g_mix, w_in_b, TM).reshape(b, s, IN_W)
    meta_proj = _in_proj(meta.astype(x.dtype), g_mix, w_in_b, N_META)
    meta_proj = jnp.pad(meta_proj, ((0, META_PAD - N_META), (0, 0)))

    slopes = jnp.exp2(-8.0 * (jnp.arange(HEADS) + 1) / HEADS).astype(_F32)
    lamvec = jnp.stack([lam_q1[0], lam_k1[0], lam_q2[0], lam_k2[0]]).astype(_F32)

    o_sb = _sb_attention(proj, meta_proj, b, s).reshape(b * s, d)
    o_df = _df_attention(proj, meta_proj, slopes, lamvec, subln_g[0][None, :], b, s).reshape(b * s, d)

    h1 = _merge(x2, g_mix, o_sb, o_df, bf(w_gate[0]), b_gate[0][None, :],
                bf(w_br_sb[0]), bf(w_br_df[0]), bf(w_out[0]))
    y = _ffn(h1, norm_ffn_g[0][None, :], bf(w_ffn_gate[0]), bf(w_ffn_up[0]), bf(w_ffn_down[0]),
             norm_final_g[None, :])
    return y.reshape(b, s, d)
```
